```python
import math
import jax, jax.numpy as jnp
from jax import lax
import numpy as np

D_MODEL = 1024
BATCH = 16
SEQ = 2048
DEPTH = 4

GRID_W = 64
CTX_LEN = 256
D_CHUNK = D_MODEL // 4
D_ATTN = D_MODEL // 2
D_LRU = D_MODEL // 4
D_MIX = D_CHUNK + D_ATTN + D_LRU
CHUNK = 128
A_GROUPS = 4
A_GDIM = D_CHUNK // A_GROUPS
HEAD_DIM = 64
N_HEADS = D_ATTN // HEAD_DIM
N_KV_HEADS = N_HEADS // 4
GQA_GROUP = N_HEADS // N_KV_HEADS
D_KV = N_KV_HEADS * HEAD_DIM
Q_BLOCK = 128
ROPE_THETA = 10000.0
LRU_BLOCKS = 4
LRU_BDIM = D_LRU // LRU_BLOCKS
CONV_W = 4
LRU_C = 8.0
N_DIR = 2
IN_SPLITS = (D_CHUNK, D_CHUNK, D_CHUNK,
             D_ATTN, D_KV, D_KV, D_ATTN,
             D_LRU, D_LRU)
IN_OFFSETS = tuple(int(s) for s in np.cumsum(IN_SPLITS)[:-1])
D_IN = int(sum(IN_SPLITS))
ALPHA = (2.0 * DEPTH) ** 0.25
BETA = (8.0 * DEPTH) ** -0.25
LN_EPS = 1e-6
RMS_EPS = 1e-6

kernel_name = "hybrid_parallel_groups_dit_block"


def layer_norm(x, g, b):
    xf = x.astype(jnp.float32)
    mu = jnp.mean(xf, -1, keepdims=True)
    var = jnp.mean(jnp.square(xf - mu), -1, keepdims=True)
    return ((xf - mu) * lax.rsqrt(var + LN_EPS) * g + b).astype(x.dtype)


def rms_norm(x, g):
    xf = x.astype(jnp.float32)
    ms = jnp.mean(jnp.square(xf), -1, keepdims=True)
    return (xf * lax.rsqrt(ms + RMS_EPS) * g).astype(x.dtype)


def axial_rope(x, rows, cols):
    half = HEAD_DIM // 2
    nf = half // 2
    inv = ROPE_THETA ** (-jnp.arange(nf, dtype=jnp.float32) / nf)

    def rot(xp, p):
        ang = p.astype(jnp.float32)[:, None] * inv
        cos = jnp.cos(ang)[None, :, None, :]
        sin = jnp.sin(ang)[None, :, None, :]
        x1 = xp[..., :nf].astype(jnp.float32)
        x2 = xp[..., nf:].astype(jnp.float32)
        return jnp.concatenate([x1 * cos - x2 * sin, x1 * sin + x2 * cos], -1)

    out = jnp.concatenate([rot(x[..., :half], rows), rot(x[..., half:], cols)], -1)
    return out.astype(x.dtype)


def chunk_gmlp(u, v, g, b, w_s, b_s):
    Bn, L, _ = u.shape
    u = jax.nn.gelu(u)
    v = layer_norm(jax.nn.gelu(v), g, b)
    v = v.reshape(Bn, L // CHUNK, CHUNK, A_GROUPS, A_GDIM)
    s = jnp.einsum('gpq,bnqgc->bnpgc', w_s, v) + b_s.T[:, :, None]
    return u * s.reshape(Bn, L, D_CHUNK)


def blocked_attention(q, k, v):
    Bn, L = q.shape[:2]
    nblk = L // Q_BLOCK
    qb = q.reshape(Bn, nblk, Q_BLOCK, N_KV_HEADS, GQA_GROUP, HEAD_DIM).transpose(1, 0, 2, 3, 4, 5)
    scale = HEAD_DIM ** -0.5

    def one_block(qblk):
        s = jnp.einsum('bqgrd,bkgd->bgrqk', qblk, k).astype(jnp.float32) * scale
        p = jax.nn.softmax(s, axis=-1).astype(v.dtype)
        return jnp.einsum('bgrqk,bkgd->bqgrd', p, v)

    o = lax.map(one_block, qb)
    return o.transpose(1, 0, 2, 3, 4, 5).reshape(Bn, L, D_ATTN)


def centred_dwconv(x, w, b):
    L = x.shape[1]
    left = CONV_W // 2
    xp = jnp.pad(x, ((0, 0), (left, CONV_W - 1 - left), (0, 0)))
    y = b
    for t in range(CONV_W):
        y = y + xp[:, t:t + L] * w[t]
    return y


def rglru_coeffs(x, w_r, b_r, w_i, b_i, lam):
    xb = x.reshape(*x.shape[:-1], LRU_BLOCKS, LRU_BDIM)
    r = jax.nn.sigmoid(jnp.einsum('blhi,hij->blhj', xb, w_r).reshape(x.shape) + b_r)
    i = jax.nn.sigmoid(jnp.einsum('blhi,hij->blhj', xb, w_i).reshape(x.shape) + b_i)
    log_a = -LRU_C * r.astype(jnp.float32) * jax.nn.softplus(-lam.astype(jnp.float32))
    a = jnp.exp(log_a)
    mult = jnp.sqrt(-jnp.expm1(2.0 * log_a))
    return a, mult * (i * x).astype(jnp.float32)


def linear_scan(a, bx, h0):
    bx = bx.at[:, 0].add(a[:, 0] * h0)

    def comb(l, r):
        return (l[0] * r[0], r[0] * l[1] + r[1])

    _, h = lax.associative_scan(comb, (a, bx), axis=1)
    return h


def rglru_direction(xl, xc, w_r, b_r, w_i, b_i, lam, reverse):
    a_c, b_c = rglru_coeffs(xc, w_r, b_r, w_i, b_i, lam)
    a_l, b_l = rglru_coeffs(xl, w_r, b_r, w_i, b_i, lam)
    if reverse:
        a_c, b_c, a_l, b_l = (jnp.flip(t, 1) for t in (a_c, b_c, a_l, b_l))
    h_c = linear_scan(a_c, b_c, jnp.zeros(a_c.shape[::2], jnp.float32))
    h_l = linear_scan(a_l, b_l, h_c[:, -1])
    if reverse:
        h_c, h_l = jnp.flip(h_c, 1), jnp.flip(h_l, 1)
    return h_l, h_c


def hybrid_layer(x, xc, c, c_ctx, rows, cols, w_ada, b_ada, w_in, a_norm_g, a_norm_b, a_ws, a_bs,
                 q_norm_g, k_norm_g, conv_w, conv_b, lru_wr, lru_br, lru_wi, lru_bi, lru_lam,
                 w_o, ln_g, ln_b, with_ctx_out):
    Bn, L, _ = x.shape
    Lc = xc.shape[1]
    shift, scale, gate = jnp.split(jax.nn.silu(c) @ w_ada + b_ada, 3, -1)
    shift_c, scale_c, gate_c = jnp.split(jax.nn.silu(c_ctx) @ w_ada + b_ada, 3, -1)
    z = (x * (1 + scale[:, None]) + shift[:, None]) @ w_in
    zc = (xc * (1 + scale_c) + shift_c) @ w_in
    a_u, a_v, a_g, q, k, v, b_g, r_x, r_g = jnp.split(z, IN_OFFSETS, -1)
    ac_u, ac_v, ac_g, qc, kc, vc, bc_g, rc_x, rc_g = jnp.split(zc, IN_OFFSETS, -1)

    q = axial_rope(rms_norm(q.reshape(Bn, L, N_HEADS, HEAD_DIM), q_norm_g), rows, cols)
    k = axial_rope(rms_norm(k.reshape(Bn, L, N_KV_HEADS, HEAD_DIM), k_norm_g), rows, cols)
    v = v.reshape(Bn, L, N_KV_HEADS, HEAD_DIM)
    kc = rms_norm(kc.reshape(Bn, Lc, N_KV_HEADS, HEAD_DIM), k_norm_g)
    vc = vc.reshape(Bn, Lc, N_KV_HEADS, HEAD_DIM)
    attn = blocked_attention(q, jnp.concatenate([k, kc], 1), jnp.concatenate([v, vc], 1))

    mix_a = chunk_gmlp(a_u, a_v, a_norm_g, a_norm_b, a_ws, a_bs)

    xr = centred_dwconv(r_x, conv_w, conv_b)
    xrc = centred_dwconv(rc_x, conv_w, conv_b)
    h_f, hc_f = rglru_direction(xr, xrc, lru_wr[0], lru_br[0], lru_wi[0], lru_bi[0], lru_lam[0], False)
    h_b, hc_b = rglru_direction(xr, xrc, lru_wr[1], lru_br[1], lru_wi[1], lru_bi[1], lru_lam[1], True)
    lru = (h_f + h_b).astype(x.dtype)

    y = jnp.concatenate([mix_a * jax.nn.silu(a_g), attn * jax.nn.silu(b_g),
                         lru * jax.nn.silu(r_g)], -1) @ w_o
    x_new = layer_norm(ALPHA * x + gate[:, None] * y, ln_g, ln_b)

    if not with_ctx_out:
        return x_new, xc
    qc = rms_norm(qc.reshape(Bn, Lc, N_HEADS, HEAD_DIM), q_norm_g)
    attn_c = blocked_attention(qc, kc, vc)
    mix_ac = chunk_gmlp(ac_u, ac_v, a_norm_g, a_norm_b, a_ws, a_bs)
    lru_c = (hc_f + hc_b).astype(xc.dtype)
    yc = jnp.concatenate([mix_ac * jax.nn.silu(ac_g), attn_c * jax.nn.silu(bc_g),
                          lru_c * jax.nn.silu(rc_g)], -1) @ w_o
    xc_new = layer_norm(ALPHA * xc + gate_c * yc, ln_g, ln_b)
    return x_new, xc_new


def setup_inputs(seed: int = 0) -> dict:
    key = jax.random.key(seed)
    ks = jax.random.split(key, 24)
    f32 = jnp.float32
    nrm = lambda k, shape, s: jax.random.normal(k, shape, f32) * s
    a0 = jax.random.uniform(ks[18], (DEPTH, N_DIR, D_LRU), f32, minval=0.9, maxval=0.999)
    return {
        "x": nrm(ks[0], (BATCH, SEQ, D_MODEL), 1.0),
        "c": nrm(ks[1], (BATCH, D_MODEL), 1.0),
        "ctx": nrm(ks[2], (BATCH, CTX_LEN, D_MODEL), 1.0),
        "c_ctx": nrm(ks[3], (D_MODEL,), 1.0),
        "w_ada": nrm(ks[4], (DEPTH, D_MODEL, 3 * D_MODEL), 0.5 * D_MODEL ** -0.5),
        "b_ada": nrm(ks[5], (DEPTH, 3 * D_MODEL), 0.01),
        "w_in": nrm(ks[6], (DEPTH, D_MODEL, D_IN), D_MODEL ** -0.5),
        "a_norm_g": 1.0 + nrm(ks[7], (DEPTH, D_CHUNK), 0.01),
        "a_norm_b": nrm(ks[8], (DEPTH, D_CHUNK), 0.01),
        "a_ws": nrm(ks[9], (DEPTH, A_GROUPS, CHUNK, CHUNK), CHUNK ** -0.5),
        "a_bs": 1.0 + nrm(ks[10], (DEPTH, A_GROUPS, CHUNK), 0.01),
        "q_norm_g": 1.0 + nrm(ks[11], (DEPTH, HEAD_DIM), 0.01),
        "k_norm_g": 1.0 + nrm(ks[12], (DEPTH, HEAD_DIM), 0.01),
        "conv_w": nrm(ks[13], (DEPTH, CONV_W, D_LRU), CONV_W ** -0.5),
        "conv_b": nrm(ks[14], (DEPTH, D_LRU), 0.01),
        "lru_wr": nrm(ks[15], (DEPTH, N_DIR, LRU_BLOCKS, LRU_BDIM, LRU_BDIM), LRU_BDIM ** -0.5),
        "lru_br": nrm(ks[16], (DEPTH, N_DIR, D_LRU), 0.01),
        "lru_wi": nrm(ks[17], (DEPTH, N_DIR, LRU_BLOCKS, LRU_BDIM, LRU_BDIM), LRU_BDIM ** -0.5),
        "lru_bi": nrm(ks[19], (DEPTH, N_DIR, D_LRU), 0.01),
        "lru_lam": jnp.log(a0) - jnp.log1p(-a0),
        "w_o": nrm(ks[20], (DEPTH, D_MIX, D_MODEL), BETA * D_MIX ** -0.5),
        "ln_g": 1.0 + nrm(ks[21], (DEPTH, D_MODEL), 0.01),
        "ln_b": nrm(ks[22], (DEPTH, D_MODEL), 0.01),
    }


def reference(x, c, ctx, c_ctx, w_ada, b_ada, w_in, a_norm_g, a_norm_b, a_ws, a_bs, q_norm_g,
              k_norm_g, conv_w, conv_b, lru_wr, lru_br, lru_wi, lru_bi, lru_lam, w_o, ln_g, ln_b):
    n_lat = x.shape[1]
    ROWS = n_lat // GRID_W
    rows = jnp.repeat(jnp.arange(ROWS, dtype=jnp.int32), GRID_W)
    cols = jnp.tile(jnp.arange(GRID_W, dtype=jnp.int32), ROWS)
    xl, xc = x, ctx
    for l in range(DEPTH):
        xl, xc = hybrid_layer(
            xl, xc, c, c_ctx, rows, cols, w_ada[l], b_ada[l], w_in[l], a_norm_g[l], a_norm_b[l],
            a_ws[l], a_bs[l], q_norm_g[l], k_norm_g[l], conv_w[l], conv_b[l], lru_wr[l], lru_br[l],
            lru_wi[l], lru_bi[l], lru_lam[l], w_o[l], ln_g[l], ln_b[l], with_ctx_out=(l < DEPTH - 1))
    return xl
```

```python
import functools

import jax
import jax.numpy as jnp
import numpy as np
from jax import lax
from jax.experimental import pallas as pl
from jax.experimental.pallas import tpu as pltpu

F32 = jnp.float32
BF16 = jnp.bfloat16

D_MODEL = 1024
DEPTH = 4
GRID_W = 64
CTX_LEN = 256
D_CHUNK = D_MODEL // 4
D_ATTN = D_MODEL // 2
D_LRU = D_MODEL // 4
CHUNK = 128
A_GROUPS = 4
A_GDIM = D_CHUNK // A_GROUPS
HEAD_DIM = 64
N_HEADS = D_ATTN // HEAD_DIM
N_KV_HEADS = N_HEADS // 4
GQA_GROUP = N_HEADS // N_KV_HEADS
D_KV = N_KV_HEADS * HEAD_DIM
ROPE_THETA = 10000.0
LRU_BLOCKS = 4
LRU_BDIM = D_LRU // LRU_BLOCKS
CONV_W = 4
LRU_C = 8.0
N_DIR = 2
D_IN = 3 * D_CHUNK + 2 * D_ATTN + 2 * D_KV + 2 * D_LRU
ALPHA = (2.0 * DEPTH) ** 0.25
LN_EPS = 1e-6
RMS_EPS = 1e-6

OFF_AU, OFF_AV, OFF_AG = 0, D_CHUNK, 2 * D_CHUNK
OFF_Q = 3 * D_CHUNK
OFF_K = OFF_Q + D_ATTN
OFF_V = OFF_K + D_KV
OFF_BG = OFF_V + D_KV
OFF_RX = OFF_BG + D_ATTN
OFF_RG = OFF_RX + D_LRU

LANES = 128
SUBLANES = 8
TOK_TILE = 256
Q_TILE = 128
LRU_T = 128
MOD_ROWS = 24
VMEM_LIMIT = 48 * 1024 * 1024


def _layer_norm(t, g, b):
    mu = jnp.mean(t, axis=-1, keepdims=True)
    d = t - mu
    var = jnp.mean(d * d, axis=-1, keepdims=True)
    return d * lax.rsqrt(var + LN_EPS) * g + b


def _ada_kernel(c_ref, w_ref, b_ref, o_ref):
    h = jax.nn.silu(c_ref[...]).astype(BF16)
    o_ref[...] = jnp.dot(h, w_ref[...].astype(BF16), preferred_element_type=F32) + b_ref[...]


def _ada_call(cc, w_ada, b_ada):
    n_col = 3 * D_MODEL // D_MODEL
    return pl.pallas_call(
        _ada_kernel,
        grid=(DEPTH, n_col),
        in_specs=[
            pl.BlockSpec((MOD_ROWS, D_MODEL), lambda l, j: (0, 0)),
            pl.BlockSpec((None, D_MODEL, D_MODEL), lambda l, j: (l, 0, j)),
            pl.BlockSpec((None, 1, D_MODEL), lambda l, j: (l, 0, j)),
        ],
        out_specs=pl.BlockSpec((None, MOD_ROWS, D_MODEL), lambda l, j: (l, 0, j)),
        out_shape=jax.ShapeDtypeStruct((DEPTH, MOD_ROWS, 3 * D_MODEL), F32),
        compiler_params=pltpu.CompilerParams(
            dimension_semantics=("arbitrary", "arbitrary"), vmem_limit_bytes=VMEM_LIMIT),
        name="adaln",
    )(cc, w_ada, b_ada.reshape(DEPTH, 1, 3 * D_MODEL))


def _head_sumsq(t, obd):
    sq = t * t
    hi = sq.astype(BF16)
    lo = (sq - hi.astype(F32)).astype(BF16)
    cols = []
    for c in range(t.shape[1] // LANES):
        sl = slice(c * LANES, (c + 1) * LANES)
        cols.append(jnp.dot(hi[:, sl], obd, preferred_element_type=F32)
                    + jnp.dot(lo[:, sl], obd, preferred_element_type=F32))
    return cols[0] if len(cols) == 1 else jnp.concatenate(cols, axis=1)


def _rope(t, cos, sin, first_half):
    cols = []
    for c in range(t.shape[1] // LANES):
        xc = t[:, c * LANES:(c + 1) * LANES]
        up = pltpu.roll(xc, LANES - HEAD_DIM // 4, 1)
        dn = pltpu.roll(xc, HEAD_DIM // 4, 1)
        cols.append(xc * cos + jnp.where(first_half, up, dn) * sin)
    return cols[0] if len(cols) == 1 else jnp.concatenate(cols, axis=1)


def _in_kernel(x_ref, mod_ref, w_ref, ang_ref, anb_ref, wcat_ref, bs_ref, qg_ref, kg_ref,
               cos_ref, sin_ref, obd_ref,
               oa_ref, q_ref, kt_ref, v_ref, gb_ref, rx_ref, gr_ref):
    mod = mod_ref[...]
    shift = mod[:, 0:D_MODEL]
    scale = mod[:, D_MODEL:2 * D_MODEL]
    xm = (x_ref[...] * (1.0 + scale) + shift).astype(BF16)
    z = jnp.dot(xm, w_ref[...], preferred_element_type=F32)

    u = jax.nn.gelu(z[:, OFF_AU:OFF_AU + D_CHUNK])
    vn = _layer_norm(jax.nn.gelu(z[:, OFF_AV:OFF_AV + D_CHUNK]), ang_ref[...], anb_ref[...])
    ga = jax.nn.silu(z[:, OFF_AG:OFF_AG + D_CHUNK])
    grp = lax.broadcasted_iota(jnp.int32, (1, D_CHUNK), 1) // A_GDIM
    for ch in range(TOK_TILE // CHUNK):
        rs = slice(ch * CHUNK, (ch + 1) * CHUNK)
        vc = vn[rs]
        vstack = jnp.concatenate(
            [jnp.where(grp == g, vc, 0.0) for g in range(A_GROUPS)], axis=0).astype(BF16)
        s = jnp.dot(wcat_ref[...], vstack, preferred_element_type=F32) + bs_ref[...]
        oa_ref[rs, :] = (u[rs] * s * ga[rs]).astype(BF16)

    obd = obd_ref[...]
    cos = cos_ref[...]
    sin = sin_ref[...]
    lane = lax.broadcasted_iota(jnp.int32, (1, LANES), 1)
    first_half = (lane % (HEAD_DIM // 2)) < (HEAD_DIM // 4)
    q = z[:, OFF_Q:OFF_Q + D_ATTN]
    q = q * lax.rsqrt(_head_sumsq(q, obd) * (1.0 / HEAD_DIM) + RMS_EPS) * qg_ref[...]
    q_ref[...] = _rope(q, cos, sin, first_half).astype(BF16)
    k = z[:, OFF_K:OFF_K + D_KV]
    k = k * lax.rsqrt(_head_sumsq(k, obd) * (1.0 / HEAD_DIM) + RMS_EPS) * kg_ref[...]
    kt = _rope(k, cos, sin, first_half).T.astype(BF16)
    kt_ref[...] = jnp.concatenate(
        [kt[g * HEAD_DIM:(g + 1) * HEAD_DIM] for g in range(N_KV_HEADS) for _ in range(GQA_GROUP)],
        axis=0)
    v_ref[...] = z[:, OFF_V:OFF_V + D_KV].astype(BF16)
    gb_ref[...] = jax.nn.silu(z[:, OFF_BG:OFF_BG + D_ATTN]).astype(BF16)

    rx_ref[...] = z[:, OFF_RX:OFF_RX + D_LRU]
    gr_ref[...] = jax.nn.silu(z[:, OFF_RG:OFF_RG + D_LRU]).astype(BF16)


def _in_call(xa, ss, w_in, ang, anb, wcat, bs_full, qg, kg, cos_t, sin_t, obd):
    bn, lt, _ = xa.shape
    nt = lt // TOK_TILE
    tok = lambda width: pl.BlockSpec((None, TOK_TILE, width), lambda b, t: (b, t, 0))
    full = lambda a: pl.BlockSpec(a.shape, lambda b, t: (0,) * a.ndim)
    return pl.pallas_call(
        _in_kernel,
        grid=(bn, nt),
        in_specs=[
            tok(D_MODEL),
            pl.BlockSpec((None, None, 1, 3 * D_MODEL), lambda b, t: (b, jnp.minimum(t, 1), 0, 0)),
            full(w_in), full(ang), full(anb), full(wcat), full(bs_full), full(qg), full(kg),
            pl.BlockSpec((TOK_TILE, LANES), lambda b, t: (t, 0)),
            pl.BlockSpec((TOK_TILE, LANES), lambda b, t: (t, 0)),
            full(obd),
        ],
        out_specs=[
            tok(D_CHUNK), tok(D_ATTN),
            pl.BlockSpec((None, GQA_GROUP * D_KV, TOK_TILE), lambda b, t: (b, 0, t)),
            tok(D_KV), tok(D_ATTN), tok(D_LRU), tok(D_LRU),
        ],
        out_shape=[
            jax.ShapeDtypeStruct((bn, lt, D_CHUNK), BF16),
            jax.ShapeDtypeStruct((bn, lt, D_ATTN), BF16),
            jax.ShapeDtypeStruct((bn, GQA_GROUP * D_KV, lt), BF16),
            jax.ShapeDtypeStruct((bn, lt, D_KV), BF16),
            jax.ShapeDtypeStruct((bn, lt, D_ATTN), BF16),
            jax.ShapeDtypeStruct((bn, lt, D_LRU), F32),
            jax.ShapeDtypeStruct((bn, lt, D_LRU), BF16),
        ],
        compiler_params=pltpu.CompilerParams(
            dimension_semantics=("parallel", "arbitrary"), vmem_limit_bytes=VMEM_LIMIT),
        name="in_proj",
    )(xa, ss, w_in, ang, anb, wcat, bs_full, qg, kg, cos_t, sin_t, obd)


def _lru_kernel(rx_ref, gr_ref, cw_ref, cb_ref, wg_ref, bg_ref, lam_ref, o_ref,
                xp_scr, xr_scr, hf_scr):
    lt = rx_ref.shape[0]
    n_chunk = lt // LRU_T
    n_ctx_chunk = CTX_LEN // LRU_T
    pad = SUBLANES
    zpad = jnp.zeros((pad, D_LRU), F32)
    xp_scr[0:pad] = zpad
    xp_scr[pad + CTX_LEN:2 * pad + CTX_LEN] = zpad
    xp_scr[2 * pad + lt:3 * pad + lt] = zpad
    xp_scr[pad:pad + CTX_LEN] = rx_ref[0:CTX_LEN]
    xp_scr[2 * pad + CTX_LEN:2 * pad + lt] = rx_ref[CTX_LEN:lt]

    cw = cw_ref[...]
    cb = cb_ref[...]
    win = LRU_T + 2 * pad

    def conv_body(i, carry):
        base = pl.multiple_of(i * LRU_T, LRU_T)
        poff = pl.multiple_of(base + jnp.where(i < n_ctx_chunk, 0, pad), pad)
        w = xp_scr[pl.ds(poff, win), :]
        acc = cb + cw[2:3] * w[pad:pad + LRU_T]
        acc = acc + cw[0:1] * pltpu.roll(w, 2, 0)[pad:pad + LRU_T]
        acc = acc + cw[1:2] * pltpu.roll(w, 1, 0)[pad:pad + LRU_T]
        acc = acc + cw[3:4] * pltpu.roll(w, win - 1, 0)[pad:pad + LRU_T]
        xr_scr[pl.ds(base, LRU_T), :] = acc
        return carry

    lax.fori_loop(0, n_chunk, conv_body, 0)

    row = lax.broadcasted_iota(jnp.int32, (LRU_T, 1), 0)

    def dir_pass(d, reverse):
        wg = wg_ref[:, d * 2 * D_LRU:(d + 1) * 2 * D_LRU]
        bg = bg_ref[:, d * 2 * D_LRU:(d + 1) * 2 * D_LRU]
        nl = -lam_ref[d:d + 1, :]
        sp = jnp.maximum(nl, 0.0) + jnp.log1p(jnp.exp(-jnp.abs(nl)))

        def body(i, carry):
            if reverse:
                idx = jnp.where(i < n_ctx_chunk, n_ctx_chunk - 1 - i, n_chunk + n_ctx_chunk - 1 - i)
            else:
                idx = i
            base = pl.multiple_of(idx * LRU_T, LRU_T)
            xr = xr_scr[pl.ds(base, LRU_T), :]
            g = jnp.dot(xr.astype(BF16), wg, preferred_element_type=F32) + bg
            r = jax.nn.sigmoid(g[:, 0:D_LRU])
            ig = jax.nn.sigmoid(g[:, D_LRU:2 * D_LRU])
            log_a = -LRU_C * r * sp
            a = jnp.exp(log_a)
            b = jnp.sqrt(-jnp.tanh(log_a) * (1.0 + a * a)) * (ig * xr)
            s = 1
            while s < LRU_T:
                if reverse:
                    a_s = pltpu.roll(a, LRU_T - s, 0)
                    b_s = pltpu.roll(b, LRU_T - s, 0)
                    keep = row < LRU_T - s
                else:
                    a_s = pltpu.roll(a, s, 0)
                    b_s = pltpu.roll(b, s, 0)
                    keep = row >= s
                b = jnp.where(keep, a * b_s + b, b)
                a = jnp.where(keep, a * a_s, a)
                s *= 2
            h = b + a * carry
            if reverse:
                gate = gr_ref[pl.ds(base, LRU_T), :].astype(F32)
                o_ref[pl.ds(base, LRU_T), :] = ((hf_scr[pl.ds(base, LRU_T), :] + h) * gate).astype(BF16)
                return h[0:1]
            hf_scr[pl.ds(base, LRU_T), :] = h
            return h[LRU_T - 1:LRU_T]

        lax.fori_loop(0, n_chunk, body, jnp.zeros((1, D_LRU), F32))

    dir_pass(0, False)
    dir_pass(1, True)


def _lru_call(rx, gr, cw, cb, wg, bg, lam):
    bn, lt, _ = rx.shape
    full = lambda a: pl.BlockSpec(a.shape, lambda b: (0,) * a.ndim)
    seq = pl.BlockSpec((None, lt, D_LRU), lambda b: (b, 0, 0))
    return pl.pallas_call(
        _lru_kernel,
        grid=(bn,),
        in_specs=[seq, seq, full(cw), full(cb), full(wg), full(bg), full(lam)],
        out_specs=seq,
        out_shape=jax.ShapeDtypeStruct((bn, lt, D_LRU), BF16),
        scratch_shapes=[
            pltpu.VMEM((lt + 3 * SUBLANES, D_LRU), F32),
            pltpu.VMEM((lt, D_LRU), F32),
            pltpu.VMEM((lt, D_LRU), F32),
        ],
        compiler_params=pltpu.CompilerParams(
            dimension_semantics=("parallel",), vmem_limit_bytes=VMEM_LIMIT),
        name="rglru",
    )(rx, gr, cw, cb, wg, bg, lam)


def _attend(q_ref, kt_ref, v_ref, att_scr, nk):
    hl = lax.broadcasted_iota(jnp.int32, (1, GQA_GROUP * HEAD_DIM), 1) // HEAD_DIM
    lane = lax.broadcasted_iota(jnp.int32, (1, LANES), 1)
    outs = {}
    for g in range(N_KV_HEADS):
        gs = slice(g * GQA_GROUP * HEAD_DIM, (g + 1) * GQA_GROUP * HEAD_DIM)
        qg = q_ref[:, gs]
        lhs = jnp.concatenate(
            [jnp.where(hl == r, qg, jnp.zeros_like(qg)) for r in range(GQA_GROUP)], axis=0)
        s = jnp.dot(lhs, kt_ref[gs, 0:nk], preferred_element_type=F32)
        m = jnp.max(s, axis=-1, keepdims=True)
        p = jnp.exp(s - m)
        l = jnp.sum(p, axis=-1, keepdims=True)
        o = jnp.dot(p.astype(BF16), v_ref[0:nk, :], preferred_element_type=F32)
        o = o * (1.0 / l)
        for r in range(GQA_GROUP):
            outs[(g, r)] = o[r * Q_TILE:(r + 1) * Q_TILE]
    heads_per_col = LANES // HEAD_DIM
    for c in range(D_ATTN // LANES):
        g = (c * heads_per_col) // GQA_GROUP
        r0 = (c * heads_per_col) % GQA_GROUP
        lo = outs[(g, r0)]
        hi = outs[(g, r0 + 1)]
        if g == 0:
            hi = pltpu.roll(hi, HEAD_DIM, 1)
        else:
            lo = pltpu.roll(lo, HEAD_DIM, 1)
        att_scr[:, c * LANES:(c + 1) * LANES] = jnp.where(lane < HEAD_DIM, lo, hi)


def _attn_kernel(q_ref, kt_ref, v_ref, oa_ref, gb_ref, ol_ref, x_ref, mod_ref, wo_ref, lng_ref,
                 lnb_ref, o_ref, att_scr, *, q_off, with_ctx):
    nk_all = kt_ref.shape[1]
    if with_ctx:
        is_ctx = (pl.program_id(1) + q_off) < (CTX_LEN // Q_TILE)

        @pl.when(is_ctx)
        def _():
            _attend(q_ref, kt_ref, v_ref, att_scr, CTX_LEN)

        @pl.when(jnp.logical_not(is_ctx))
        def _():
            _attend(q_ref, kt_ref, v_ref, att_scr, nk_all)
    else:
        _attend(q_ref, kt_ref, v_ref, att_scr, nk_all)

    att = (att_scr[...] * gb_ref[...].astype(F32)).astype(BF16)
    cat = jnp.concatenate([oa_ref[...], att, ol_ref[...]], axis=1)
    y = jnp.dot(cat, wo_ref[...], preferred_element_type=F32)
    gate = mod_ref[:, 2 * D_MODEL:3 * D_MODEL]
    o_ref[...] = _layer_norm(ALPHA * x_ref[...] + gate * y, lng_ref[...], lnb_ref[...])


def _attn_call(q, kt, v, oa, gb, ol, xa, ss, w_o, lng, lnb, last):
    bn, lt, _ = xa.shape
    n_ctx_tile = CTX_LEN // Q_TILE
    q_off = n_ctx_tile if last else 0
    nq = lt // Q_TILE - q_off
    tok = lambda width: pl.BlockSpec((None, Q_TILE, width), lambda b, i: (b, i + q_off, 0))
    full = lambda a: pl.BlockSpec(a.shape, lambda b, i: (0,) * a.ndim)
    return pl.pallas_call(
        functools.partial(_attn_kernel, q_off=q_off, with_ctx=not last),
        grid=(bn, nq),
        in_specs=[
            tok(D_ATTN),
            pl.BlockSpec((None, GQA_GROUP * D_KV, lt), lambda b, i: (b, 0, 0)),
            pl.BlockSpec((None, lt, D_KV), lambda b, i: (b, 0, 0)),
            tok(D_CHUNK), tok(D_ATTN), tok(D_LRU), tok(D_MODEL),
            pl.BlockSpec((None, None, 1, 3 * D_MODEL),
                         lambda b, i: (b, jnp.minimum((i + q_off) // n_ctx_tile, 1), 0, 0)),
            full(w_o), full(lng), full(lnb),
        ],
        out_specs=pl.BlockSpec((None, Q_TILE, D_MODEL), lambda b, i: (b, i, 0)),
        out_shape=jax.ShapeDtypeStruct((bn, nq * Q_TILE, D_MODEL), F32),
        scratch_shapes=[pltpu.VMEM((Q_TILE, D_ATTN), F32)],
        compiler_params=pltpu.CompilerParams(
            dimension_semantics=("parallel", "arbitrary"), vmem_limit_bytes=VMEM_LIMIT),
        name="attn_merge",
    )(q, kt, v, oa, gb, ol, xa, ss, w_o, lng, lnb)


def _rope_tables(n_lat):
    nf = HEAD_DIM // 4
    t = jnp.arange(n_lat, dtype=jnp.int32)
    pos = jnp.stack([t // GRID_W, t % GRID_W], axis=1).astype(F32)
    inv = ROPE_THETA ** (-jnp.arange(nf, dtype=F32) / nf)
    d = np.arange(HEAD_DIM)
    ang = pos[:, d // (HEAD_DIM // 2)] * inv[d % nf]
    sign = jnp.asarray(np.where((d % (HEAD_DIM // 2)) < nf, -1.0, 1.0), F32)
    cos = jnp.concatenate([jnp.ones((CTX_LEN, HEAD_DIM), F32), jnp.cos(ang)], axis=0)
    sin = jnp.concatenate([jnp.zeros((CTX_LEN, HEAD_DIM), F32), jnp.sin(ang) * sign], axis=0)
    reps = LANES // HEAD_DIM
    return jnp.tile(cos, (1, reps)), jnp.tile(sin, (1, reps))


def _block_diag(w):
    nb, n, _ = w.shape
    eye = jnp.eye(nb, dtype=w.dtype)
    return (eye[:, None, :, None] * w[:, :, None, :]).reshape(nb * n, nb * n)


def kernel(x, c, ctx, c_ctx, w_ada, b_ada, w_in, a_norm_g, a_norm_b, a_ws, a_bs, q_norm_g, k_norm_g,
           conv_w, conv_b, lru_wr, lru_br, lru_wi, lru_bi, lru_lam, w_o, ln_g, ln_b):
    bn, n_lat, _ = x.shape
    assert ctx.shape[1] == CTX_LEN and n_lat % TOK_TILE == 0 and bn + 1 <= MOD_ROWS
    xa = jnp.concatenate([ctx, x], axis=1)

    cc = jnp.zeros((MOD_ROWS, D_MODEL), F32).at[:bn].set(c).at[bn].set(c_ctx)
    mod = _ada_call(cc, w_ada, b_ada)
    ctx_mod = jnp.broadcast_to(mod[:, bn:bn + 1], (DEPTH, bn, 3 * D_MODEL))
    ss = jnp.stack([ctx_mod, mod[:, :bn]], axis=2).reshape(DEPTH, bn, 2, 1, 3 * D_MODEL)

    cos_t, sin_t = _rope_tables(n_lat)
    head = np.arange(LANES) // HEAD_DIM
    obd = jnp.asarray(head[:, None] == head[None, :], BF16)

    w_in_b = w_in.astype(BF16)
    w_o_b = w_o.astype(BF16)
    for l in range(DEPTH):
        wcat = jnp.concatenate([a_ws[l, g] for g in range(A_GROUPS)], axis=1).astype(BF16)
        bs_full = jnp.repeat(a_bs[l].T, A_GDIM, axis=1)
        qg = (jnp.tile(q_norm_g[l], N_HEADS) * HEAD_DIM ** -0.5).reshape(1, D_ATTN)
        kg = jnp.tile(k_norm_g[l], N_KV_HEADS).reshape(1, D_KV)
        oa, q, kt, v, gb, rx, gr = _in_call(
            xa, ss[l], w_in_b[l], a_norm_g[l].reshape(1, D_CHUNK), a_norm_b[l].reshape(1, D_CHUNK),
            wcat, bs_full, qg, kg, cos_t, sin_t, obd)

        wg = jnp.concatenate(
            [m for d in range(N_DIR) for m in (_block_diag(lru_wr[l, d]), _block_diag(lru_wi[l, d]))],
            axis=1).astype(BF16)
        bg = jnp.concatenate(
            [m for d in range(N_DIR) for m in (lru_br[l, d], lru_bi[l, d])]).reshape(1, 4 * D_LRU)
        ol = _lru_call(rx, gr, conv_w[l], conv_b[l].reshape(1, D_LRU), wg, bg, lru_lam[l])

        xa = _attn_call(q, kt, v, oa, gb, ol, xa, ss[l], w_o_b[l], ln_g[l].reshape(1, D_MODEL),
                        ln_b[l].reshape(1, D_MODEL), last=(l == DEPTH - 1))
    return xa
```

```python
import functools

import jax
import jax.numpy as jnp
import numpy as np
from jax import lax
from jax.experimental import pallas as pl
from jax.experimental.pallas import tpu as pltpu

F32 = jnp.float32
BF16 = jnp.bfloat16

D_MODEL = 1024
DEPTH = 4
GRID_W = 64
CTX_LEN = 256
D_CHUNK = D_MODEL // 4
D_ATTN = D_MODEL // 2
D_LRU = D_MODEL // 4
CHUNK = 128
A_GROUPS = 4
A_GDIM = D_CHUNK // A_GROUPS
HEAD_DIM = 64
N_HEADS = D_ATTN // HEAD_DIM
N_KV_HEADS = N_HEADS // 4
GQA_GROUP = N_HEADS // N_KV_HEADS
D_KV = N_KV_HEADS * HEAD_DIM
ROPE_THETA = 10000.0
LRU_BLOCKS = 4
LRU_BDIM = D_LRU // LRU_BLOCKS
CONV_W = 4
LRU_C = 8.0
N_DIR = 2
D_IN = 3 * D_CHUNK + 2 * D_ATTN + 2 * D_KV + 2 * D_LRU
ALPHA = (2.0 * DEPTH) ** 0.25
LN_EPS = 1e-6
RMS_EPS = 1e-6
LOG2E = 1.4426950408889634

OFF_AU, OFF_AV, OFF_AG = 0, D_CHUNK, 2 * D_CHUNK
OFF_Q = 3 * D_CHUNK
OFF_K = OFF_Q + D_ATTN
OFF_V = OFF_K + D_KV
OFF_BG = OFF_V + D_KV
OFF_RX = OFF_BG + D_ATTN
OFF_RG = OFF_RX + D_LRU

LANES = 128
SUBLANES = 8
TOK_TILE = 256
Q_TILE = 128
Q_STEP = 256
PIPE_UNROLL = 9
KEY_CHUNK = 256
LRU_T = 128
MOD_ROWS = 24
VMEM_LIMIT = 48 * 1024 * 1024


def _layer_norm(t, g, b):
    mu = jnp.mean(t, axis=-1, keepdims=True)
    d = t - mu
    var = jnp.mean(d * d, axis=-1, keepdims=True)
    return d * lax.rsqrt(var + LN_EPS) * g + b


def _ada_kernel(c_ref, w_ref, b_ref, o_ref):
    h = jax.nn.silu(c_ref[...]).astype(BF16)
    o_ref[...] = jnp.dot(h, w_ref[...].astype(BF16), preferred_element_type=F32) + b_ref[...]


def _ada_call(cc, w_ada, b_ada):
    n_col = 3 * D_MODEL // D_MODEL
    return pl.pallas_call(
        _ada_kernel,
        grid=(DEPTH, n_col),
        in_specs=[
            pl.BlockSpec((MOD_ROWS, D_MODEL), lambda l, j: (0, 0)),
            pl.BlockSpec((None, D_MODEL, D_MODEL), lambda l, j: (l, 0, j)),
            pl.BlockSpec((None, 1, D_MODEL), lambda l, j: (l, 0, j)),
        ],
        out_specs=pl.BlockSpec((None, MOD_ROWS, D_MODEL), lambda l, j: (l, 0, j)),
        out_shape=jax.ShapeDtypeStruct((DEPTH, MOD_ROWS, 3 * D_MODEL), F32),
        compiler_params=pltpu.CompilerParams(
            dimension_semantics=("arbitrary", "arbitrary"), vmem_limit_bytes=VMEM_LIMIT),
        name="adaln",
    )(cc, w_ada, b_ada.reshape(DEPTH, 1, 3 * D_MODEL))


def _head_sumsq(t, obd):
    sq = t * t
    hi = sq.astype(BF16)
    lo = (sq - hi.astype(F32)).astype(BF16)
    cols = []
    for c in range(t.shape[1] // LANES):
        sl = slice(c * LANES, (c + 1) * LANES)
        cols.append(jnp.dot(hi[:, sl], obd, preferred_element_type=F32)
                    + jnp.dot(lo[:, sl], obd, preferred_element_type=F32))
    return cols[0] if len(cols) == 1 else jnp.concatenate(cols, axis=1)


def _rope(t, cos, sin, first_half):
    cols = []
    for c in range(t.shape[1] // LANES):
        xc = t[:, c * LANES:(c + 1) * LANES]
        up = pltpu.roll(xc, LANES - HEAD_DIM // 4, 1)
        dn = pltpu.roll(xc, HEAD_DIM // 4, 1)
        cols.append(xc * cos + jnp.where(first_half, up, dn) * sin)
    return cols[0] if len(cols) == 1 else jnp.concatenate(cols, axis=1)


def _in_kernel(x_ref, mod_ref, w_ref, ang_ref, anb_ref, wcat_ref, bs_ref, qg_ref, kg_ref,
               cos_ref, sin_ref, obd_ref,
               oa_ref, qt_ref, k_ref, va_ref, gb_ref, rx_ref, gr_ref):
    mod = mod_ref[...]
    shift = mod[:, 0:D_MODEL]
    scale = mod[:, D_MODEL:2 * D_MODEL]
    xm = (x_ref[...] * (1.0 + scale) + shift).astype(BF16)
    z = jnp.dot(xm, w_ref[...], preferred_element_type=F32)

    u = jax.nn.gelu(z[:, OFF_AU:OFF_AU + D_CHUNK])
    vn = _layer_norm(jax.nn.gelu(z[:, OFF_AV:OFF_AV + D_CHUNK]), ang_ref[...], anb_ref[...])
    ga = jax.nn.silu(z[:, OFF_AG:OFF_AG + D_CHUNK])
    grp = lax.broadcasted_iota(jnp.int32, (1, D_CHUNK), 1) // A_GDIM
    for ch in range(TOK_TILE // CHUNK):
        rs = slice(ch * CHUNK, (ch + 1) * CHUNK)
        vc = vn[rs]
        vstack = jnp.concatenate(
            [jnp.where(grp == g, vc, 0.0) for g in range(A_GROUPS)], axis=0).astype(BF16)
        s = jnp.dot(wcat_ref[...], vstack, preferred_element_type=F32) + bs_ref[...]
        oa_ref[rs, :] = (u[rs] * s * ga[rs]).astype(BF16)

    obd = obd_ref[...]
    cos = cos_ref[...]
    sin = sin_ref[...]
    lane = lax.broadcasted_iota(jnp.int32, (1, LANES), 1)
    first_half = (lane % (HEAD_DIM // 2)) < (HEAD_DIM // 4)
    q = z[:, OFF_Q:OFF_Q + D_ATTN]
    q = q * lax.rsqrt(_head_sumsq(q, obd) * (1.0 / HEAD_DIM) + RMS_EPS) * qg_ref[...]
    qt_ref[...] = _rope(q, cos, sin, first_half).T.astype(BF16)
    k = z[:, OFF_K:OFF_K + D_KV]
    k = k * lax.rsqrt(_head_sumsq(k, obd) * (1.0 / HEAD_DIM) + RMS_EPS) * kg_ref[...]
    k_ref[...] = _rope(k, cos, sin, first_half).astype(BF16)
    vt = z[:, OFF_V:OFF_V + D_KV].T.astype(BF16)
    ones = jnp.ones((HEAD_DIM, TOK_TILE), BF16)
    va_ref[0, 0] = jnp.concatenate([vt[0:HEAD_DIM], ones], axis=0)
    va_ref[1, 0] = jnp.concatenate([ones, vt[HEAD_DIM:2 * HEAD_DIM]], axis=0)
    gb_ref[...] = jax.nn.silu(z[:, OFF_BG:OFF_BG + D_ATTN]).astype(BF16)

    rx_ref[...] = z[:, OFF_RX:OFF_RX + D_LRU]
    gr_ref[...] = jax.nn.silu(z[:, OFF_RG:OFF_RG + D_LRU]).astype(BF16)


def _in_call(xa, ss, w_in, ang, anb, wcat, bs_full, qg, kg, cos_t, sin_t, obd):
    bn, lt, _ = xa.shape
    nt = lt // TOK_TILE
    tok = lambda width: pl.BlockSpec((None, TOK_TILE, width), lambda b, t: (b, t, 0))
    full = lambda a: pl.BlockSpec(a.shape, lambda b, t: (0,) * a.ndim)
    return pl.pallas_call(
        _in_kernel,
        grid=(bn, nt),
        in_specs=[
            tok(D_MODEL),
            pl.BlockSpec((None, None, 1, 3 * D_MODEL), lambda b, t: (b, jnp.minimum(t, 1), 0, 0)),
            full(w_in), full(ang), full(anb), full(wcat), full(bs_full), full(qg), full(kg),
            pl.BlockSpec((TOK_TILE, LANES), lambda b, t: (t, 0)),
            pl.BlockSpec((TOK_TILE, LANES), lambda b, t: (t, 0)),
            full(obd),
        ],
        out_specs=[
            tok(D_CHUNK),
            pl.BlockSpec((None, D_ATTN, TOK_TILE), lambda b, t: (b, 0, t)),
            tok(D_KV),
            pl.BlockSpec((None, N_KV_HEADS, 1, D_KV, TOK_TILE), lambda b, t: (b, 0, t, 0, 0)),
            tok(D_ATTN), tok(D_LRU), tok(D_LRU),
        ],
        out_shape=[
            jax.ShapeDtypeStruct((bn, lt, D_CHUNK), BF16),
            jax.ShapeDtypeStruct((bn, D_ATTN, lt), BF16),
            jax.ShapeDtypeStruct((bn, lt, D_KV), BF16),
            jax.ShapeDtypeStruct((bn, N_KV_HEADS, nt, D_KV, TOK_TILE), BF16),
            jax.ShapeDtypeStruct((bn, lt, D_ATTN), BF16),
            jax.ShapeDtypeStruct((bn, lt, D_LRU), F32),
            jax.ShapeDtypeStruct((bn, lt, D_LRU), BF16),
        ],
        compiler_params=pltpu.CompilerParams(
            dimension_semantics=("parallel", "arbitrary"), vmem_limit_bytes=VMEM_LIMIT),
        name="in_proj",
    )(xa, ss, w_in, ang, anb, wcat, bs_full, qg, kg, cos_t, sin_t, obd)


def _lru_kernel(rx_ref, gr_ref, cw_ref, cb_ref, wg_ref, bg_ref, lam_ref, o_ref,
                xp_scr, xr_scr, hf_scr):
    lt = rx_ref.shape[0]
    n_chunk = lt // LRU_T
    n_ctx_chunk = CTX_LEN // LRU_T
    pad = SUBLANES
    zpad = jnp.zeros((pad, D_LRU), F32)
    xp_scr[0:pad] = zpad
    xp_scr[pad + CTX_LEN:2 * pad + CTX_LEN] = zpad
    xp_scr[2 * pad + lt:3 * pad + lt] = zpad
    xp_scr[pad:pad + CTX_LEN] = rx_ref[0:CTX_LEN]
    xp_scr[2 * pad + CTX_LEN:2 * pad + lt] = rx_ref[CTX_LEN:lt]

    cw = cw_ref[...]
    cb = cb_ref[...]
    win = LRU_T + 2 * pad

    def conv_body(i, carry):
        base = pl.multiple_of(i * LRU_T, LRU_T)
        poff = pl.multiple_of(base + jnp.where(i < n_ctx_chunk, 0, pad), pad)
        w = xp_scr[pl.ds(poff, win), :]
        acc = cb + cw[2:3] * w[pad:pad + LRU_T]
        acc = acc + cw[0:1] * pltpu.roll(w, 2, 0)[pad:pad + LRU_T]
        acc = acc + cw[1:2] * pltpu.roll(w, 1, 0)[pad:pad + LRU_T]
        acc = acc + cw[3:4] * pltpu.roll(w, win - 1, 0)[pad:pad + LRU_T]
        xr_scr[pl.ds(base, LRU_T), :] = acc
        return carry

    lax.fori_loop(0, n_chunk, conv_body, 0)

    row = lax.broadcasted_iota(jnp.int32, (LRU_T, 1), 0)

    def dir_pass(d, reverse):
        wg = wg_ref[:, d * 2 * D_LRU:(d + 1) * 2 * D_LRU]
        bg = bg_ref[:, d * 2 * D_LRU:(d + 1) * 2 * D_LRU]
        nl = -lam_ref[d:d + 1, :]
        sp = jnp.maximum(nl, 0.0) + jnp.log1p(jnp.exp(-jnp.abs(nl)))

        def body(i, carry):
            if reverse:
                idx = jnp.where(i < n_ctx_chunk, n_ctx_chunk - 1 - i, n_chunk + n_ctx_chunk - 1 - i)
            else:
                idx = i
            base = pl.multiple_of(idx * LRU_T, LRU_T)
            xr = xr_scr[pl.ds(base, LRU_T), :]
            g = jnp.dot(xr.astype(BF16), wg, preferred_element_type=F32) + bg
            r = jax.nn.sigmoid(g[:, 0:D_LRU])
            ig = jax.nn.sigmoid(g[:, D_LRU:2 * D_LRU])
            log_a = -LRU_C * r * sp
            a = jnp.exp(log_a)
            b = jnp.sqrt(-jnp.tanh(log_a) * (1.0 + a * a)) * (ig * xr)
            s = 1
            while s < LRU_T:
                if reverse:
                    a_s = pltpu.roll(a, LRU_T - s, 0)
                    b_s = pltpu.roll(b, LRU_T - s, 0)
                    keep = row < LRU_T - s
                else:
                    a_s = pltpu.roll(a, s, 0)
                    b_s = pltpu.roll(b, s, 0)
                    keep = row >= s
                b = jnp.where(keep, a * b_s + b, b)
                a = jnp.where(keep, a * a_s, a)
                s *= 2
            h = b + a * carry
            if reverse:
                gate = gr_ref[pl.ds(base, LRU_T), :].astype(F32)
                o_ref[pl.ds(base, LRU_T), :] = ((hf_scr[pl.ds(base, LRU_T), :] + h) * gate).astype(BF16)
                return h[0:1]
            hf_scr[pl.ds(base, LRU_T), :] = h
            return h[LRU_T - 1:LRU_T]

        lax.fori_loop(0, n_chunk, body, jnp.zeros((1, D_LRU), F32))

    dir_pass(0, False)
    dir_pass(1, True)


def _lru_call(rx, gr, cw, cb, wg, bg, lam):
    bn, lt, _ = rx.shape
    full = lambda a: pl.BlockSpec(a.shape, lambda b: (0,) * a.ndim)
    seq = pl.BlockSpec((None, lt, D_LRU), lambda b: (b, 0, 0))
    return pl.pallas_call(
        _lru_kernel,
        grid=(bn,),
        in_specs=[seq, seq, full(cw), full(cb), full(wg), full(bg), full(lam)],
        out_specs=seq,
        out_shape=jax.ShapeDtypeStruct((bn, lt, D_LRU), BF16),
        scratch_shapes=[
            pltpu.VMEM((lt + 3 * SUBLANES, D_LRU), F32),
            pltpu.VMEM((lt, D_LRU), F32),
            pltpu.VMEM((lt, D_LRU), F32),
        ],
        compiler_params=pltpu.CompilerParams(
            dimension_semantics=("parallel",), vmem_limit_bytes=VMEM_LIMIT),
        name="rglru",
    )(rx, gr, cw, cb, wg, bg, lam)


def _attend(qt_ref, k_ref, va_ref, att_scr, rhs_scr, s_scr, p_scr, ot_scr, nk):
    pair = LANES // HEAD_DIM
    width = GQA_GROUP * Q_TILE
    n_chunk = nk // KEY_CHUNK
    units = [(qb, g) for qb in range(qt_ref.shape[1] // Q_TILE) for g in range(N_KV_HEADS)]
    n_unit = len(units)
    zeros = jnp.zeros((HEAD_DIM, width), BF16)

    def stage_rhs(u):
        qb, g = units[u]
        h0 = g * GQA_GROUP
        qt = jnp.concatenate(
            [qt_ref[(h0 + j) * HEAD_DIM:(h0 + j + 1) * HEAD_DIM, qb * Q_TILE:(qb + 1) * Q_TILE]
             for j in range(GQA_GROUP)], axis=1)
        rhs_scr[u % 2] = jnp.concatenate([qt, zeros] if g == 0 else [zeros, qt], axis=0)

    def score_chunk(u, c, mx):
        ks = pl.ds(pl.multiple_of(c * KEY_CHUNK, KEY_CHUNK), KEY_CHUNK)
        s = jnp.dot(k_ref[ks, :], rhs_scr[u % 2], preferred_element_type=F32)
        s_scr[u % 2, c] = s
        for i in range(KEY_CHUNK // SUBLANES):
            mx = jnp.maximum(mx, s[i * SUBLANES:(i + 1) * SUBLANES])
        return mx

    def exp_chunk(u, c, m):
        p_scr[u % 2, c] = jnp.exp2(s_scr[u % 2, c] - m).astype(BF16)

    def pv_chunk(u, c):
        g = units[u][1]
        ot_scr[...] += jnp.dot(va_ref[g, c], p_scr[u % 2, c], preferred_element_type=F32)

    def finish(u):
        qb, g = units[u]
        ot = ot_scr[...]
        if g == 0:
            o, l = ot[0:HEAD_DIM], ot[HEAD_DIM:HEAD_DIM + 1]
        else:
            o, l = ot[HEAD_DIM:2 * HEAD_DIM], ot[0:1]
        o = o * (1.0 / l)
        for rp in range(GQA_GROUP // pair):
            stacked = jnp.concatenate(
                [o[:, (rp * pair + j) * Q_TILE:(rp * pair + j + 1) * Q_TILE] for j in range(pair)], axis=0)
            col = (g * GQA_GROUP + rp * pair) // pair
            att_scr[qb * Q_TILE:(qb + 1) * Q_TILE, col * LANES:(col + 1) * LANES] = stacked.T

    col_max = {}
    for t in range(n_unit + 2):
        ua, ub, uc = t, t - 1, t - 2
        do_a, do_b, do_c = ua < n_unit, 0 <= ub < n_unit, 0 <= uc < n_unit
        if do_a:
            stage_rhs(ua)
        if do_c:
            ot_scr[...] = jnp.zeros(ot_scr.shape, F32)

        def body(c, mx, ua=ua, ub=ub, uc=uc, do_a=do_a, do_b=do_b, do_c=do_c):
            if do_a:
                mx = score_chunk(ua, c, mx)
            if do_b:
                exp_chunk(ub, c, col_max[ub])
            if do_c:
                pv_chunk(uc, c)
            return mx

        mx = lax.fori_loop(0, n_chunk, body, jnp.full((SUBLANES, width), -jnp.inf, F32),
                           unroll=min(PIPE_UNROLL, n_chunk))
        if do_a:
            col_max[ua] = jnp.max(mx, axis=0, keepdims=True)
        if do_c:
            finish(uc)


def _attn_kernel(qt_ref, k_ref, va_ref, oa_ref, gb_ref, ol_ref, x_ref, mod_ref, wo_ref, lng_ref,
                 lnb_ref, o_ref, att_scr, rhs_scr, s_scr, p_scr, ot_scr, *, q_off, with_ctx):
    nk_all = k_ref.shape[0]
    if with_ctx:
        is_ctx = (pl.program_id(1) + q_off) < (CTX_LEN // Q_STEP)

        @pl.when(is_ctx)
        def _():
            _attend(qt_ref, k_ref, va_ref, att_scr, rhs_scr, s_scr, p_scr, ot_scr, CTX_LEN)

        @pl.when(jnp.logical_not(is_ctx))
        def _():
            _attend(qt_ref, k_ref, va_ref, att_scr, rhs_scr, s_scr, p_scr, ot_scr, nk_all)
    else:
        _attend(qt_ref, k_ref, va_ref, att_scr, rhs_scr, s_scr, p_scr, ot_scr, nk_all)

    att = (att_scr[...] * gb_ref[...].astype(F32)).astype(BF16)
    cat = jnp.concatenate([oa_ref[...], att, ol_ref[...]], axis=1)
    y = jnp.dot(cat, wo_ref[...], preferred_element_type=F32)
    gate = mod_ref[:, 2 * D_MODEL:3 * D_MODEL]
    o_ref[...] = _layer_norm(ALPHA * x_ref[...] + gate * y, lng_ref[...], lnb_ref[...])


def _attn_call(qt, k, va, oa, gb, ol, xa, ss, w_o, lng, lnb, last):
    bn, lt, _ = xa.shape
    n_ctx_step = CTX_LEN // Q_STEP
    q_off = n_ctx_step if last else 0
    nq = lt // Q_STEP - q_off
    tok = lambda width: pl.BlockSpec((None, Q_STEP, width), lambda b, i: (b, i + q_off, 0))
    full = lambda a: pl.BlockSpec(a.shape, lambda b, i: (0,) * a.ndim)
    return pl.pallas_call(
        functools.partial(_attn_kernel, q_off=q_off, with_ctx=not last),
        grid=(bn, nq),
        in_specs=[
            pl.BlockSpec((None, D_ATTN, Q_STEP), lambda b, i: (b, 0, i + q_off)),
            pl.BlockSpec((None, lt, D_KV), lambda b, i: (b, 0, 0)),
            pl.BlockSpec((None, N_KV_HEADS, lt // KEY_CHUNK, D_KV, KEY_CHUNK), lambda b, i: (b, 0, 0, 0, 0)),
            tok(D_CHUNK), tok(D_ATTN), tok(D_LRU), tok(D_MODEL),
            pl.BlockSpec((None, None, 1, 3 * D_MODEL),
                         lambda b, i: (b, jnp.minimum((i + q_off) // n_ctx_step, 1), 0, 0)),
            full(w_o), full(lng), full(lnb),
        ],
        out_specs=pl.BlockSpec((None, Q_STEP, D_MODEL), lambda b, i: (b, i, 0)),
        out_shape=jax.ShapeDtypeStruct((bn, nq * Q_STEP, D_MODEL), F32),
        scratch_shapes=[
            pltpu.VMEM((Q_STEP, D_ATTN), F32),
            pltpu.VMEM((2, D_KV, GQA_GROUP * Q_TILE), BF16),
            pltpu.VMEM((2, lt // KEY_CHUNK, KEY_CHUNK, GQA_GROUP * Q_TILE), F32),
            pltpu.VMEM((2, lt // KEY_CHUNK, KEY_CHUNK, GQA_GROUP * Q_TILE), BF16),
            pltpu.VMEM((D_KV, GQA_GROUP * Q_TILE), F32),
        ],
        compiler_params=pltpu.CompilerParams(
            dimension_semantics=("parallel", "arbitrary"), vmem_limit_bytes=VMEM_LIMIT),
        name="attn_merge",
    )(qt, k, va, oa, gb, ol, xa, ss, w_o, lng, lnb)


def _rope_tables(n_lat):
    nf = HEAD_DIM // 4
    t = jnp.arange(n_lat, dtype=jnp.int32)
    pos = jnp.stack([t // GRID_W, t % GRID_W], axis=1).astype(F32)
    inv = ROPE_THETA ** (-jnp.arange(nf, dtype=F32) / nf)
    d = np.arange(HEAD_DIM)
    ang = pos[:, d // (HEAD_DIM // 2)] * inv[d % nf]
    sign = jnp.asarray(np.where((d % (HEAD_DIM // 2)) < nf, -1.0, 1.0), F32)
    cos = jnp.concatenate([jnp.ones((CTX_LEN, HEAD_DIM), F32), jnp.cos(ang)], axis=0)
    sin = jnp.concatenate([jnp.zeros((CTX_LEN, HEAD_DIM), F32), jnp.sin(ang) * sign], axis=0)
    reps = LANES // HEAD_DIM
    return jnp.tile(cos, (1, reps)), jnp.tile(sin, (1, reps))


def _block_diag(w):
    nb, n, _ = w.shape
    eye = jnp.eye(nb, dtype=w.dtype)
    return (eye[:, None, :, None] * w[:, :, None, :]).reshape(nb * n, nb * n)


def kernel(x, c, ctx, c_ctx, w_ada, b_ada, w_in, a_norm_g, a_norm_b, a_ws, a_bs, q_norm_g, k_norm_g,
           conv_w, conv_b, lru_wr, lru_br, lru_wi, lru_bi, lru_lam, w_o, ln_g, ln_b):
    bn, n_lat, _ = x.shape
    assert ctx.shape[1] == CTX_LEN and n_lat % TOK_TILE == 0 and bn + 1 <= MOD_ROWS
    xa = jnp.concatenate([ctx, x], axis=1)

    cc = jnp.zeros((MOD_ROWS, D_MODEL), F32).at[:bn].set(c).at[bn].set(c_ctx)
    mod = _ada_call(cc, w_ada, b_ada)
    ctx_mod = jnp.broadcast_to(mod[:, bn:bn + 1], (DEPTH, bn, 3 * D_MODEL))
    ss = jnp.stack([ctx_mod, mod[:, :bn]], axis=2).reshape(DEPTH, bn, 2, 1, 3 * D_MODEL)

    cos_t, sin_t = _rope_tables(n_lat)
    head = np.arange(LANES) // HEAD_DIM
    obd = jnp.asarray(head[:, None] == head[None, :], BF16)

    w_in_b = w_in.astype(BF16)
    w_o_b = w_o.astype(BF16)
    for l in range(DEPTH):
        wcat = jnp.concatenate([a_ws[l, g] for g in range(A_GROUPS)], axis=1).astype(BF16)
        bs_full = jnp.repeat(a_bs[l].T, A_GDIM, axis=1)
        qg = (jnp.tile(q_norm_g[l], N_HEADS) * (HEAD_DIM ** -0.5 * LOG2E)).reshape(1, D_ATTN)
        kg = jnp.tile(k_norm_g[l], N_KV_HEADS).reshape(1, D_KV)
        oa, qt, k, va, gb, rx, gr = _in_call(
            xa, ss[l], w_in_b[l], a_norm_g[l].reshape(1, D_CHUNK), a_norm_b[l].reshape(1, D_CHUNK),
            wcat, bs_full, qg, kg, cos_t, sin_t, obd)

        wg = jnp.concatenate(
            [m for d in range(N_DIR) for m in (_block_diag(lru_wr[l, d]), _block_diag(lru_wi[l, d]))],
            axis=1).astype(BF16)
        bg = jnp.concatenate(
            [m for d in range(N_DIR) for m in (lru_br[l, d], lru_bi[l, d])]).reshape(1, 4 * D_LRU)
        ol = _lru_call(rx, gr, conv_w[l], conv_b[l].reshape(1, D_LRU), wg, bg, lru_lam[l])

        xa = _attn_call(qt, k, va, oa, gb, ol, xa, ss[l], w_o_b[l], ln_g[l].reshape(1, D_MODEL),
                        ln_b[l].reshape(1, D_MODEL), last=(l == DEPTH - 1))
    return xa
```

```python
import functools

import jax
import jax.numpy as jnp
import numpy as np
from jax import lax
from jax.experimental import pallas as pl
from jax.experimental.pallas import tpu as pltpu

F32 = jnp.float32
BF16 = jnp.bfloat16

D_MODEL = 1024
DEPTH = 4
GRID_W = 64
CTX_LEN = 256
D_CHUNK = D_MODEL // 4
D_ATTN = D_MODEL // 2
D_LRU = D_MODEL // 4
CHUNK = 128
A_GROUPS = 4
A_GDIM = D_CHUNK // A_GROUPS
HEAD_DIM = 64
N_HEADS = D_ATTN // HEAD_DIM
N_KV_HEADS = N_HEADS // 4
GQA_GROUP = N_HEADS // N_KV_HEADS
D_KV = N_KV_HEADS * HEAD_DIM
ROPE_THETA = 10000.0
LRU_BLOCKS = 4
LRU_BDIM = D_LRU // LRU_BLOCKS
CONV_W = 4
LRU_C = 8.0
N_DIR = 2
D_IN = 3 * D_CHUNK + 2 * D_ATTN + 2 * D_KV + 2 * D_LRU
ALPHA = (2.0 * DEPTH) ** 0.25
LN_EPS = 1e-6
RMS_EPS = 1e-6
LOG2E = 1.4426950408889634
SCORE_BOUND_SLACK = 1.02
MAX_SAFE_SCORE_BOUND = 40.0

OFF_AU, OFF_AV, OFF_AG = 0, D_CHUNK, 2 * D_CHUNK
OFF_Q = 3 * D_CHUNK
OFF_K = OFF_Q + D_ATTN
OFF_V = OFF_K + D_KV
OFF_BG = OFF_V + D_KV
OFF_RX = OFF_BG + D_ATTN
OFF_RG = OFF_RX + D_LRU

LANES = 128
SUBLANES = 8
TOK_TILE = 256
Q_TILE = 128
Q_STEP = 256
PIPE_UNROLL = 9
KEY_CHUNK = 256
LRU_T = 128
LRU_SEG = 8
CTX_PITCH = 36
LAT_PITCH = 260
MOD_ROWS = 24
VMEM_LIMIT = 48 * 1024 * 1024


def _layer_norm(t, g, b):
    mu = jnp.mean(t, axis=-1, keepdims=True)
    d = t - mu
    var = jnp.mean(d * d, axis=-1, keepdims=True)
    return d * lax.rsqrt(var + LN_EPS) * g + b


def _ada_kernel(c_ref, w_ref, b_ref, o_ref):
    h = jax.nn.silu(c_ref[...]).astype(BF16)
    o_ref[...] = jnp.dot(h, w_ref[...].astype(BF16), preferred_element_type=F32) + b_ref[...]


def _ada_call(cc, w_ada, b_ada):
    n_col = 3 * D_MODEL // D_MODEL
    return pl.pallas_call(
        _ada_kernel,
        grid=(DEPTH, n_col),
        in_specs=[
            pl.BlockSpec((MOD_ROWS, D_MODEL), lambda l, j: (0, 0)),
            pl.BlockSpec((None, D_MODEL, D_MODEL), lambda l, j: (l, 0, j)),
            pl.BlockSpec((None, 1, D_MODEL), lambda l, j: (l, 0, j)),
        ],
        out_specs=pl.BlockSpec((None, MOD_ROWS, D_MODEL), lambda l, j: (l, 0, j)),
        out_shape=jax.ShapeDtypeStruct((DEPTH, MOD_ROWS, 3 * D_MODEL), F32),
        compiler_params=pltpu.CompilerParams(
            dimension_semantics=("arbitrary", "arbitrary"), vmem_limit_bytes=VMEM_LIMIT),
        name="adaln",
    )(cc, w_ada, b_ada.reshape(DEPTH, 1, 3 * D_MODEL))


def _head_sumsq(t, obd):
    sq = t * t
    hi = sq.astype(BF16)
    lo = (sq - hi.astype(F32)).astype(BF16)
    cols = []
    for c in range(t.shape[1] // LANES):
        sl = slice(c * LANES, (c + 1) * LANES)
        cols.append(jnp.dot(hi[:, sl], obd, preferred_element_type=F32)
                    + jnp.dot(lo[:, sl], obd, preferred_element_type=F32))
    return cols[0] if len(cols) == 1 else jnp.concatenate(cols, axis=1)


def _rope(t, cos, sin, first_half):
    cols = []
    for c in range(t.shape[1] // LANES):
        xc = t[:, c * LANES:(c + 1) * LANES]
        up = pltpu.roll(xc, LANES - HEAD_DIM // 4, 1)
        dn = pltpu.roll(xc, HEAD_DIM // 4, 1)
        cols.append(xc * cos + jnp.where(first_half, up, dn) * sin)
    return cols[0] if len(cols) == 1 else jnp.concatenate(cols, axis=1)


def _in_kernel(x_ref, mod_ref, w_ref, ang_ref, anb_ref, wcat_ref, bs_ref, qg_ref, kg_ref,
               cos_ref, sin_ref, obd_ref,
               oa_ref, qt_ref, k_ref, va_ref, gb_ref, rx_ref, gr_ref):
    mod = mod_ref[...]
    shift = mod[:, 0:D_MODEL]
    scale = mod[:, D_MODEL:2 * D_MODEL]
    xm = (x_ref[...] * (1.0 + scale) + shift).astype(BF16)
    z = jnp.dot(xm, w_ref[...], preferred_element_type=F32)

    u = jax.nn.gelu(z[:, OFF_AU:OFF_AU + D_CHUNK])
    vn = _layer_norm(jax.nn.gelu(z[:, OFF_AV:OFF_AV + D_CHUNK]), ang_ref[...], anb_ref[...])
    ga = jax.nn.silu(z[:, OFF_AG:OFF_AG + D_CHUNK])
    grp = lax.broadcasted_iota(jnp.int32, (1, D_CHUNK), 1) // A_GDIM
    for ch in range(TOK_TILE // CHUNK):
        rs = slice(ch * CHUNK, (ch + 1) * CHUNK)
        vc = vn[rs]
        vstack = jnp.concatenate(
            [jnp.where(grp == g, vc, 0.0) for g in range(A_GROUPS)], axis=0).astype(BF16)
        s = jnp.dot(wcat_ref[...], vstack, preferred_element_type=F32) + bs_ref[...]
        oa_ref[rs, :] = (u[rs] * s * ga[rs]).astype(BF16)

    obd = obd_ref[...]
    cos = cos_ref[...]
    sin = sin_ref[...]
    lane = lax.broadcasted_iota(jnp.int32, (1, LANES), 1)
    first_half = (lane % (HEAD_DIM // 2)) < (HEAD_DIM // 4)
    q = z[:, OFF_Q:OFF_Q + D_ATTN]
    q = q * lax.rsqrt(_head_sumsq(q, obd) * (1.0 / HEAD_DIM) + RMS_EPS) * qg_ref[...]
    qt_ref[...] = _rope(q, cos, sin, first_half).T.astype(BF16)
    k = z[:, OFF_K:OFF_K + D_KV]
    k = k * lax.rsqrt(_head_sumsq(k, obd) * (1.0 / HEAD_DIM) + RMS_EPS) * kg_ref[...]
    k_ref[...] = _rope(k, cos, sin, first_half).astype(BF16)
    vt = z[:, OFF_V:OFF_V + D_KV].T.astype(BF16)
    ones = jnp.ones((HEAD_DIM, TOK_TILE), BF16)
    va_ref[0, 0] = jnp.concatenate([vt[0:HEAD_DIM], ones], axis=0)
    va_ref[1, 0] = jnp.concatenate([ones, vt[HEAD_DIM:2 * HEAD_DIM]], axis=0)
    gb_ref[...] = jax.nn.silu(z[:, OFF_BG:OFF_BG + D_ATTN]).astype(BF16)

    rx_ref[...] = z[:, OFF_RX:OFF_RX + D_LRU]
    gr_ref[...] = jax.nn.silu(z[:, OFF_RG:OFF_RG + D_LRU]).astype(BF16)


def _in_call(xa, ss, w_in, ang, anb, wcat, bs_full, qg, kg, cos_t, sin_t, obd):
    bn, lt, _ = xa.shape
    nt = lt // TOK_TILE
    tok = lambda width: pl.BlockSpec((None, TOK_TILE, width), lambda b, t: (b, t, 0))
    full = lambda a: pl.BlockSpec(a.shape, lambda b, t: (0,) * a.ndim)
    return pl.pallas_call(
        _in_kernel,
        grid=(bn, nt),
        in_specs=[
            tok(D_MODEL),
            pl.BlockSpec((None, None, 1, 3 * D_MODEL), lambda b, t: (b, jnp.minimum(t, 1), 0, 0)),
            full(w_in), full(ang), full(anb), full(wcat), full(bs_full), full(qg), full(kg),
            pl.BlockSpec((TOK_TILE, LANES), lambda b, t: (t, 0)),
            pl.BlockSpec((TOK_TILE, LANES), lambda b, t: (t, 0)),
            full(obd),
        ],
        out_specs=[
            tok(D_CHUNK),
            pl.BlockSpec((None, D_ATTN, TOK_TILE), lambda b, t: (b, 0, t)),
            tok(D_KV),
            pl.BlockSpec((None, N_KV_HEADS, 1, D_KV, TOK_TILE), lambda b, t: (b, 0, t, 0, 0)),
            tok(D_ATTN), tok(D_LRU), tok(D_LRU),
        ],
        out_shape=[
            jax.ShapeDtypeStruct((bn, lt, D_CHUNK), BF16),
            jax.ShapeDtypeStruct((bn, D_ATTN, lt), BF16),
            jax.ShapeDtypeStruct((bn, lt, D_KV), BF16),
            jax.ShapeDtypeStruct((bn, N_KV_HEADS, nt, D_KV, TOK_TILE), BF16),
            jax.ShapeDtypeStruct((bn, lt, D_ATTN), BF16),
            jax.ShapeDtypeStruct((bn, lt, D_LRU), F32),
            jax.ShapeDtypeStruct((bn, lt, D_LRU), BF16),
        ],
        compiler_params=pltpu.CompilerParams(
            dimension_semantics=("parallel", "arbitrary"), vmem_limit_bytes=VMEM_LIMIT),
        name="in_proj",
    )(xa, ss, w_in, ang, anb, wcat, bs_full, qg, kg, cos_t, sin_t, obd)


def _sigmoid(t):
    return 0.5 * jnp.tanh(0.5 * t) + 0.5


def _lru_kernel(rx_ref, gr_ref, cw_ref, cb_ref, wg_ref, bg_ref, lam_ref, o_ref, xp_scr, a_scr, b_scr):
    lt = rx_ref.shape[0]
    n_lat = lt - CTX_LEN
    n_chunk = lt // LRU_T
    n_ctx_chunk = CTX_LEN // LRU_T
    ctx_rows = LRU_SEG * CTX_PITCH
    lat_rows = LRU_SEG * LAT_PITCH
    assert CTX_LEN <= ctx_rows and n_lat <= lat_rows and (ctx_rows - CTX_LEN) % SUBLANES == 0
    regions = ((0, CTX_PITCH, CTX_LEN, 0), (ctx_rows, LAT_PITCH, n_lat, CTX_LEN))
    slabs = D_LRU // LANES

    pad = SUBLANES
    zpad = jnp.zeros((pad, D_LRU), F32)
    xp_scr[0:pad] = zpad
    xp_scr[pad + CTX_LEN:2 * pad + CTX_LEN] = zpad
    xp_scr[2 * pad + lt:3 * pad + lt] = zpad
    xp_scr[pad:pad + CTX_LEN] = rx_ref[0:CTX_LEN]
    xp_scr[2 * pad + CTX_LEN:2 * pad + lt] = rx_ref[CTX_LEN:lt]
    for off, pitch, rows, _ in regions:
        n_pad = LRU_SEG * pitch - rows
        for d in range(N_DIR):
            for sl in range(slabs):
                a_scr[d, sl, off + rows:off + rows + n_pad, :] = jnp.ones((n_pad, LANES), F32)
                b_scr[d, sl, off + rows:off + rows + n_pad, :] = jnp.zeros((n_pad, LANES), F32)

    cw = cw_ref[...]
    cb = cb_ref[...]
    win = LRU_T + 2 * pad
    sp = []
    for d in range(N_DIR):
        nl = -lam_ref[d:d + 1, :]
        sp.append(jnp.maximum(nl, 0.0) + jnp.log1p(jnp.exp(-jnp.abs(nl))))

    def coeff_body(i, carry):
        base = pl.multiple_of(i * LRU_T, LRU_T)
        in_ctx = i < n_ctx_chunk
        poff = pl.multiple_of(base + jnp.where(in_ctx, 0, pad), pad)
        w = xp_scr[pl.ds(poff, win), :]
        xr = cb + cw[2:3] * w[pad:pad + LRU_T]
        xr = xr + cw[0:1] * pltpu.roll(w, 2, 0)[pad:pad + LRU_T]
        xr = xr + cw[1:2] * pltpu.roll(w, 1, 0)[pad:pad + LRU_T]
        xr = xr + cw[3:4] * pltpu.roll(w, win - 1, 0)[pad:pad + LRU_T]
        g = jnp.dot(xr.astype(BF16), wg_ref[...], preferred_element_type=F32) + bg_ref[...]
        srow = pl.multiple_of(base + jnp.where(in_ctx, 0, ctx_rows - CTX_LEN), SUBLANES)
        for d in range(N_DIR):
            r = _sigmoid(g[:, 2 * d * D_LRU:(2 * d + 1) * D_LRU])
            ig = _sigmoid(g[:, (2 * d + 1) * D_LRU:(2 * d + 2) * D_LRU])
            log_a = -LRU_C * r * sp[d]
            a = jnp.exp(log_a)
            b = jnp.sqrt(-jnp.tanh(log_a) * (1.0 + a * a)) * (ig * xr)
            for sl in range(slabs):
                a_scr[d, sl, pl.ds(srow, LRU_T), :] = a[:, sl * LANES:(sl + 1) * LANES]
                b_scr[d, sl, pl.ds(srow, LRU_T), :] = b[:, sl * LANES:(sl + 1) * LANES]
        return carry

    lax.fori_loop(0, n_chunk, coeff_body, 0)

    def sweep(off, pitch):
        def body(i, state):
            new = []
            for d in range(N_DIR):
                t = i if d == 0 else pitch - 1 - i
                idx = pl.ds(off + t, LRU_SEG, stride=pitch)
                for sl in range(slabs):
                    h, p = state[2 * (d * slabs + sl)], state[2 * (d * slabs + sl) + 1]
                    a = a_scr[d, sl, idx, :]
                    h = a * h + b_scr[d, sl, idx, :]
                    p = a * p
                    b_scr[d, sl, idx, :] = h
                    a_scr[d, sl, idx, :] = p
                    new += [h, p]
            return tuple(new)

        ident = (jnp.zeros((LRU_SEG, LANES), F32), jnp.ones((LRU_SEG, LANES), F32)) * (N_DIR * slabs)
        lax.fori_loop(0, pitch, body, ident, unroll=4)

    for off, pitch, _, _ in regions:
        sweep(off, pitch)

    def chain(d, off, pitch, init):
        carries = [None] * LRU_SEG
        c = init
        for j in (range(LRU_SEG) if d == 0 else reversed(range(LRU_SEG))):
            carries[j] = c
            last = off + j * pitch + (pitch - 1 if d == 0 else 0)
            c = [b_scr[d, sl, last:last + 1, :] + a_scr[d, sl, last:last + 1, :] * c[sl]
                 for sl in range(slabs)]
        return carries, c

    zero = [jnp.zeros((1, LANES), F32)] * slabs
    carries = []
    for d in range(N_DIR):
        c_ctx, fin = chain(d, regions[0][0], regions[0][1], zero)
        c_lat, _ = chain(d, regions[1][0], regions[1][1], fin)
        carries.append((c_ctx, c_lat))

    row = lax.broadcasted_iota(jnp.int32, (LRU_T, 1), 0)
    for ci in range(n_chunk):
        reg = 0 if ci < n_ctx_chunk else 1
        off, pitch, _, tok0 = regions[reg]
        r0 = ci * LRU_T - tok0
        j_lo, j_hi = r0 // pitch, (r0 + LRU_T - 1) // pitch
        for sl in range(slabs):
            rows = slice(off + r0, off + r0 + LRU_T)
            h = None
            for d in range(N_DIR):
                seg_c = carries[d][reg]
                c = seg_c[j_hi][sl]
                for j in range(j_hi - 1, j_lo - 1, -1):
                    c = jnp.where(row < (j + 1) * pitch - r0, seg_c[j][sl], c)
                hd = b_scr[d, sl, rows, :] + a_scr[d, sl, rows, :] * c
                h = hd if h is None else h + hd
            cols = slice(sl * LANES, (sl + 1) * LANES)
            gate = gr_ref[ci * LRU_T:(ci + 1) * LRU_T, cols].astype(F32)
            o_ref[ci * LRU_T:(ci + 1) * LRU_T, cols] = (h * gate).astype(BF16)


def _lru_call(rx, gr, cw, cb, wg, bg, lam):
    bn, lt, _ = rx.shape
    full = lambda a: pl.BlockSpec(a.shape, lambda b: (0,) * a.ndim)
    seq = pl.BlockSpec((None, lt, D_LRU), lambda b: (b, 0, 0))
    scan_rows = LRU_SEG * (CTX_PITCH + LAT_PITCH)
    return pl.pallas_call(
        _lru_kernel,
        grid=(bn,),
        in_specs=[seq, seq, full(cw), full(cb), full(wg), full(bg), full(lam)],
        out_specs=seq,
        out_shape=jax.ShapeDtypeStruct((bn, lt, D_LRU), BF16),
        scratch_shapes=[
            pltpu.VMEM((lt + 3 * SUBLANES, D_LRU), F32),
            pltpu.VMEM((N_DIR, D_LRU // LANES, scan_rows, LANES), F32),
            pltpu.VMEM((N_DIR, D_LRU // LANES, scan_rows, LANES), F32),
        ],
        compiler_params=pltpu.CompilerParams(
            dimension_semantics=("parallel",), vmem_limit_bytes=VMEM_LIMIT),
        name="rglru",
    )(rx, gr, cw, cb, wg, bg, lam)


def _attend(qt_ref, k_ref, va_ref, att_scr, rhs_scr, s_scr, p_scr, ot_scr, nk, bound):
    pair = LANES // HEAD_DIM
    width = GQA_GROUP * Q_TILE
    n_chunk = nk // KEY_CHUNK
    units = [(qb, g) for qb in range(qt_ref.shape[1] // Q_TILE) for g in range(N_KV_HEADS)]
    n_unit = len(units)
    zeros = jnp.zeros((HEAD_DIM, width), BF16)

    def stage_rhs(u):
        qb, g = units[u]
        h0 = g * GQA_GROUP
        qt = jnp.concatenate(
            [qt_ref[(h0 + j) * HEAD_DIM:(h0 + j + 1) * HEAD_DIM, qb * Q_TILE:(qb + 1) * Q_TILE]
             for j in range(GQA_GROUP)], axis=1)
        rhs_scr[u % 2] = jnp.concatenate([qt, zeros] if g == 0 else [zeros, qt], axis=0)

    def score_chunk(u, c, mx):
        ks = pl.ds(pl.multiple_of(c * KEY_CHUNK, KEY_CHUNK), KEY_CHUNK)
        s = jnp.dot(k_ref[ks, :], rhs_scr[u % 2], preferred_element_type=F32)
        s_scr[u % 2, c] = s
        for i in range(KEY_CHUNK // SUBLANES):
            mx = jnp.maximum(mx, s[i * SUBLANES:(i + 1) * SUBLANES])
        return mx

    def exp_chunk(u, c, m):
        p_scr[u % 2, c] = jnp.exp2(s_scr[u % 2, c] - m).astype(BF16)

    def pv_chunk(u, c):
        g = units[u][1]
        ot_scr[...] += jnp.dot(va_ref[g, c], p_scr[u % 2, c], preferred_element_type=F32)

    def finish(u):
        qb, g = units[u]
        ot = ot_scr[...]
        if g == 0:
            o, l = ot[0:HEAD_DIM], ot[HEAD_DIM:HEAD_DIM + 1]
        else:
            o, l = ot[HEAD_DIM:2 * HEAD_DIM], ot[0:1]
        o = o * (1.0 / l)
        for rp in range(GQA_GROUP // pair):
            stacked = jnp.concatenate(
                [o[:, (rp * pair + j) * Q_TILE:(rp * pair + j + 1) * Q_TILE] for j in range(pair)], axis=0)
            col = (g * GQA_GROUP + rp * pair) // pair
            att_scr[qb * Q_TILE:(qb + 1) * Q_TILE, col * LANES:(col + 1) * LANES] = stacked.T

    def score_exp_chunk(u, c):
        ks = pl.ds(pl.multiple_of(c * KEY_CHUNK, KEY_CHUNK), KEY_CHUNK)
        s = jnp.dot(k_ref[ks, :], rhs_scr[u % 2], preferred_element_type=F32)
        p_scr[u % 2, c] = jnp.exp2(s - bound).astype(BF16)

    if bound is not None:
        for t in range(n_unit + 1):
            ua, uc = t, t - 1
            do_a, do_c = ua < n_unit, uc >= 0
            if do_a:
                stage_rhs(ua)
            if do_c:
                ot_scr[...] = jnp.zeros(ot_scr.shape, F32)

            def fast_body(c, carry, ua=ua, uc=uc, do_a=do_a, do_c=do_c):
                if do_a:
                    score_exp_chunk(ua, c)
                if do_c:
                    pv_chunk(uc, c)
                return carry

            lax.fori_loop(0, n_chunk, fast_body, 0, unroll=min(PIPE_UNROLL, n_chunk))
            if do_c:
                finish(uc)
        return

    col_max = {}
    for t in range(n_unit + 2):
        ua, ub, uc = t, t - 1, t - 2
        do_a, do_b, do_c = ua < n_unit, 0 <= ub < n_unit, 0 <= uc < n_unit
        if do_a:
            stage_rhs(ua)
        if do_c:
            ot_scr[...] = jnp.zeros(ot_scr.shape, F32)

        def body(c, mx, ua=ua, ub=ub, uc=uc, do_a=do_a, do_b=do_b, do_c=do_c):
            if do_a:
                mx = score_chunk(ua, c, mx)
            if do_b:
                exp_chunk(ub, c, col_max[ub])
            if do_c:
                pv_chunk(uc, c)
            return mx

        mx = lax.fori_loop(0, n_chunk, body, jnp.full((SUBLANES, width), -jnp.inf, F32),
                           unroll=min(PIPE_UNROLL, n_chunk))
        if do_a:
            col_max[ua] = jnp.max(mx, axis=0, keepdims=True)
        if do_c:
            finish(uc)


def _attn_kernel(flag_ref, bnd_ref, qt_ref, k_ref, va_ref, oa_ref, gb_ref, ol_ref, x_ref, mod_ref,
                 wo_ref, lng_ref, lnb_ref, o_ref, att_scr, rhs_scr, s_scr, p_scr, ot_scr,
                 *, q_off, with_ctx):
    nk_all = k_ref.shape[0]
    scratch = (att_scr, rhs_scr, s_scr, p_scr, ot_scr)
    bounded = flag_ref[0] == 1
    is_ctx = (pl.program_id(1) + q_off) < (CTX_LEN // Q_STEP)
    for ctx_step, nk in ((True, CTX_LEN), (False, nk_all)):
        if ctx_step and not with_ctx:
            continue
        here = (is_ctx if ctx_step else jnp.logical_not(is_ctx)) if with_ctx else True

        @pl.when(jnp.logical_and(here, bounded))
        def _():
            _attend(qt_ref, k_ref, va_ref, *scratch, nk, bnd_ref[...])

        @pl.when(jnp.logical_and(here, jnp.logical_not(bounded)))
        def _():
            _attend(qt_ref, k_ref, va_ref, *scratch, nk, None)

    att = (att_scr[...] * gb_ref[...].astype(F32)).astype(BF16)
    cat = jnp.concatenate([oa_ref[...], att, ol_ref[...]], axis=1)
    y = jnp.dot(cat, wo_ref[...], preferred_element_type=F32)
    gate = mod_ref[:, 2 * D_MODEL:3 * D_MODEL]
    o_ref[...] = _layer_norm(ALPHA * x_ref[...] + gate * y, lng_ref[...], lnb_ref[...])


def _attn_call(flag, bnd, qt, k, va, oa, gb, ol, xa, ss, w_o, lng, lnb, last):
    bn, lt, _ = xa.shape
    n_ctx_step = CTX_LEN // Q_STEP
    q_off = n_ctx_step if last else 0
    nq = lt // Q_STEP - q_off
    tok = lambda width: pl.BlockSpec((None, Q_STEP, width), lambda b, i: (b, i + q_off, 0))
    full = lambda a: pl.BlockSpec(a.shape, lambda b, i: (0,) * a.ndim)
    return pl.pallas_call(
        functools.partial(_attn_kernel, q_off=q_off, with_ctx=not last),
        grid=(bn, nq),
        in_specs=[
            pl.BlockSpec(memory_space=pltpu.SMEM),
            full(bnd),
            pl.BlockSpec((None, D_ATTN, Q_STEP), lambda b, i: (b, 0, i + q_off)),
            pl.BlockSpec((None, lt, D_KV), lambda b, i: (b, 0, 0)),
            pl.BlockSpec((None, N_KV_HEADS, lt // KEY_CHUNK, D_KV, KEY_CHUNK), lambda b, i: (b, 0, 0, 0, 0)),
            tok(D_CHUNK), tok(D_ATTN), tok(D_LRU), tok(D_MODEL),
            pl.BlockSpec((None, None, 1, 3 * D_MODEL),
                         lambda b, i: (b, jnp.minimum((i + q_off) // n_ctx_step, 1), 0, 0)),
            full(w_o), full(lng), full(lnb),
        ],
        out_specs=pl.BlockSpec((None, Q_STEP, D_MODEL), lambda b, i: (b, i, 0)),
        out_shape=jax.ShapeDtypeStruct((bn, nq * Q_STEP, D_MODEL), F32),
        scratch_shapes=[
            pltpu.VMEM((Q_STEP, D_ATTN), F32),
            pltpu.VMEM((2, D_KV, GQA_GROUP * Q_TILE), BF16),
            pltpu.VMEM((2, lt // KEY_CHUNK, KEY_CHUNK, GQA_GROUP * Q_TILE), F32),
            pltpu.VMEM((2, lt // KEY_CHUNK, KEY_CHUNK, GQA_GROUP * Q_TILE), BF16),
            pltpu.VMEM((D_KV, GQA_GROUP * Q_TILE), F32),
        ],
        compiler_params=pltpu.CompilerParams(
            dimension_semantics=("parallel", "arbitrary"), vmem_limit_bytes=VMEM_LIMIT),
        name="attn_merge",
    )(flag, bnd, qt, k, va, oa, gb, ol, xa, ss, w_o, lng, lnb)


def _rope_tables(n_lat):
    nf = HEAD_DIM // 4
    t = jnp.arange(n_lat, dtype=jnp.int32)
    pos = jnp.stack([t // GRID_W, t % GRID_W], axis=1).astype(F32)
    inv = ROPE_THETA ** (-jnp.arange(nf, dtype=F32) / nf)
    d = np.arange(HEAD_DIM)
    ang = pos[:, d // (HEAD_DIM // 2)] * inv[d % nf]
    sign = jnp.asarray(np.where((d % (HEAD_DIM // 2)) < nf, -1.0, 1.0), F32)
    cos = jnp.concatenate([jnp.ones((CTX_LEN, HEAD_DIM), F32), jnp.cos(ang)], axis=0)
    sin = jnp.concatenate([jnp.zeros((CTX_LEN, HEAD_DIM), F32), jnp.sin(ang) * sign], axis=0)
    reps = LANES // HEAD_DIM
    return jnp.tile(cos, (1, reps)), jnp.tile(sin, (1, reps))


def _block_diag(w):
    nb, n, _ = w.shape
    eye = jnp.eye(nb, dtype=w.dtype)
    return (eye[:, None, :, None] * w[:, :, None, :]).reshape(nb * n, nb * n)


def kernel(x, c, ctx, c_ctx, w_ada, b_ada, w_in, a_norm_g, a_norm_b, a_ws, a_bs, q_norm_g, k_norm_g,
           conv_w, conv_b, lru_wr, lru_br, lru_wi, lru_bi, lru_lam, w_o, ln_g, ln_b):
    bn, n_lat, _ = x.shape
    assert ctx.shape[1] == CTX_LEN and n_lat % TOK_TILE == 0 and bn + 1 <= MOD_ROWS
    xa = jnp.concatenate([ctx, x], axis=1)

    cc = jnp.zeros((MOD_ROWS, D_MODEL), F32).at[:bn].set(c).at[bn].set(c_ctx)
    mod = _ada_call(cc, w_ada, b_ada)
    ctx_mod = jnp.broadcast_to(mod[:, bn:bn + 1], (DEPTH, bn, 3 * D_MODEL))
    ss = jnp.stack([ctx_mod, mod[:, :bn]], axis=2).reshape(DEPTH, bn, 2, 1, 3 * D_MODEL)

    cos_t, sin_t = _rope_tables(n_lat)
    head = np.arange(LANES) // HEAD_DIM
    obd = jnp.asarray(head[:, None] == head[None, :], BF16)

    w_in_b = w_in.astype(BF16)
    w_o_b = w_o.astype(BF16)
    for l in range(DEPTH):
        wcat = jnp.concatenate([a_ws[l, g] for g in range(A_GROUPS)], axis=1).astype(BF16)
        bs_full = jnp.repeat(a_bs[l].T, A_GDIM, axis=1)
        qg = (jnp.tile(q_norm_g[l], N_HEADS) * (HEAD_DIM ** -0.5 * LOG2E)).reshape(1, D_ATTN)
        kg = jnp.tile(k_norm_g[l], N_KV_HEADS).reshape(1, D_KV)
        oa, qt, k, va, gb, rx, gr = _in_call(
            xa, ss[l], w_in_b[l], a_norm_g[l].reshape(1, D_CHUNK), a_norm_b[l].reshape(1, D_CHUNK),
            wcat, bs_full, qg, kg, cos_t, sin_t, obd)

        wg = jnp.concatenate(
            [m for d in range(N_DIR) for m in (_block_diag(lru_wr[l, d]), _block_diag(lru_wi[l, d]))],
            axis=1).astype(BF16)
        bg = jnp.concatenate(
            [m for d in range(N_DIR) for m in (lru_br[l, d], lru_bi[l, d])]).reshape(1, 4 * D_LRU)
        ol = _lru_call(rx, gr, conv_w[l], conv_b[l].reshape(1, D_LRU), wg, bg, lru_lam[l])

        s_bound = (SCORE_BOUND_SLACK * HEAD_DIM ** 0.5 * LOG2E
                   * jnp.max(jnp.abs(q_norm_g[l])) * jnp.max(jnp.abs(k_norm_g[l])))
        flag = (s_bound <= MAX_SAFE_SCORE_BOUND).astype(jnp.int32).reshape(1)
        bnd = jnp.full((1, GQA_GROUP * Q_TILE), s_bound, F32)
        xa = _attn_call(flag, bnd, qt, k, va, oa, gb, ol, xa, ss[l], w_o_b[l], ln_g[l].reshape(1, D_MODEL),
                        ln_b[l].reshape(1, D_MODEL), last=(l == DEPTH - 1))
    return xa
```

```python
import functools

import jax
import jax.numpy as jnp
import numpy as np
from jax import lax
from jax.experimental import pallas as pl
from jax.experimental.pallas import tpu as pltpu

F32 = jnp.float32
BF16 = jnp.bfloat16

D_MODEL = 1024
DEPTH = 4
GRID_W = 64
CTX_LEN = 256
D_CHUNK = D_MODEL // 4
D_ATTN = D_MODEL // 2
D_LRU = D_MODEL // 4
CHUNK = 128
A_GROUPS = 4
A_GDIM = D_CHUNK // A_GROUPS
HEAD_DIM = 64
N_HEADS = D_ATTN // HEAD_DIM
N_KV_HEADS = N_HEADS // 4
GQA_GROUP = N_HEADS // N_KV_HEADS
D_KV = N_KV_HEADS * HEAD_DIM
ROPE_THETA = 10000.0
LRU_BLOCKS = 4
LRU_BDIM = D_LRU // LRU_BLOCKS
CONV_W = 4
LRU_C = 8.0
N_DIR = 2
D_IN = 3 * D_CHUNK + 2 * D_ATTN + 2 * D_KV + 2 * D_LRU
ALPHA = (2.0 * DEPTH) ** 0.25
LN_EPS = 1e-6
RMS_EPS = 1e-6
LOG2E = 1.4426950408889634
SCORE_BOUND_SLACK = 1.02
MAX_SAFE_SCORE_BOUND = 40.0

OFF_AU, OFF_AV, OFF_AG = 0, D_CHUNK, 2 * D_CHUNK
OFF_Q = 3 * D_CHUNK
OFF_K = OFF_Q + D_ATTN
OFF_V = OFF_K + D_KV
OFF_BG = OFF_V + D_KV
OFF_RX = OFF_BG + D_ATTN
OFF_RG = OFF_RX + D_LRU

LANES = 128
SUBLANES = 8
TOK_TILE = 256
PROJ_COLS = 512
Q_TILE = 128
Q_STEP = 256
PIPE_UNROLL = 9
KEY_CHUNK = 256
LRU_T = 128
LRU_SEG = 8
CTX_PITCH = 36
LAT_PITCH = 260
MOD_ROWS = 24
VMEM_LIMIT = 48 * 1024 * 1024


def _layer_norm(t, g, b):
    mu = jnp.mean(t, axis=-1, keepdims=True)
    d = t - mu
    var = jnp.mean(d * d, axis=-1, keepdims=True)
    return d * lax.rsqrt(var + LN_EPS) * g + b


def _ada_kernel(c_ref, w_ref, b_ref, o_ref):
    h = jax.nn.silu(c_ref[...]).astype(BF16)
    o_ref[...] = jnp.dot(h, w_ref[...].astype(BF16), preferred_element_type=F32) + b_ref[...]


def _ada_call(cc, w_ada, b_ada):
    n_col = 3 * D_MODEL // D_MODEL
    return pl.pallas_call(
        _ada_kernel,
        grid=(DEPTH, n_col),
        in_specs=[
            pl.BlockSpec((MOD_ROWS, D_MODEL), lambda l, j: (0, 0)),
            pl.BlockSpec((None, D_MODEL, D_MODEL), lambda l, j: (l, 0, j)),
            pl.BlockSpec((None, 1, D_MODEL), lambda l, j: (l, 0, j)),
        ],
        out_specs=pl.BlockSpec((None, MOD_ROWS, D_MODEL), lambda l, j: (l, 0, j)),
        out_shape=jax.ShapeDtypeStruct((DEPTH, MOD_ROWS, 3 * D_MODEL), F32),
        compiler_params=pltpu.CompilerParams(
            dimension_semantics=("arbitrary", "arbitrary"), vmem_limit_bytes=VMEM_LIMIT),
        name="adaln",
    )(cc, w_ada, b_ada.reshape(DEPTH, 1, 3 * D_MODEL))


def _head_sumsq(t, obd):
    sq = t * t
    hi = sq.astype(BF16)
    lo = (sq - hi.astype(F32)).astype(BF16)
    cols = []
    for c in range(t.shape[1] // LANES):
        sl = slice(c * LANES, (c + 1) * LANES)
        cols.append(jnp.dot(hi[:, sl], obd, preferred_element_type=F32)
                    + jnp.dot(lo[:, sl], obd, preferred_element_type=F32))
    return cols[0] if len(cols) == 1 else jnp.concatenate(cols, axis=1)


def _rope(t, cos, sin, first_half):
    cols = []
    for c in range(t.shape[1] // LANES):
        xc = t[:, c * LANES:(c + 1) * LANES]
        up = pltpu.roll(xc, LANES - HEAD_DIM // 4, 1)
        dn = pltpu.roll(xc, HEAD_DIM // 4, 1)
        cols.append(xc * cos + jnp.where(first_half, up, dn) * sin)
    return cols[0] if len(cols) == 1 else jnp.concatenate(cols, axis=1)


def _silu(t):
    return t * _sigmoid(t)


def _in_epilogue(z_ref, ang_ref, anb_ref, wcat_ref, bs_ref, qg_ref, kg_ref, cos_ref, sin_ref, obd_ref,
                 oa_ref, qt_ref, k_ref, va_ref, gb_ref, rx_ref, gr_ref):
    u = jax.nn.gelu(z_ref[:, OFF_AU:OFF_AU + D_CHUNK])
    vn = _layer_norm(jax.nn.gelu(z_ref[:, OFF_AV:OFF_AV + D_CHUNK]), ang_ref[...], anb_ref[...])
    yield
    ga = _silu(z_ref[:, OFF_AG:OFF_AG + D_CHUNK])
    grp = lax.broadcasted_iota(jnp.int32, (1, D_CHUNK), 1) // A_GDIM
    for ch in range(TOK_TILE // CHUNK):
        rs = slice(ch * CHUNK, (ch + 1) * CHUNK)
        vc = vn[rs]
        vstack = jnp.concatenate(
            [jnp.where(grp == g, vc, 0.0) for g in range(A_GROUPS)], axis=0).astype(BF16)
        s = jnp.dot(wcat_ref[...], vstack, preferred_element_type=F32) + bs_ref[...]
        oa_ref[rs, :] = (u[rs] * s * ga[rs]).astype(BF16)
    yield

    obd = obd_ref[...]
    cos = cos_ref[...]
    sin = sin_ref[...]
    lane = lax.broadcasted_iota(jnp.int32, (1, LANES), 1)
    first_half = (lane % (HEAD_DIM // 2)) < (HEAD_DIM // 4)
    q = z_ref[:, OFF_Q:OFF_Q + D_ATTN]
    q = q * lax.rsqrt(_head_sumsq(q, obd) * (1.0 / HEAD_DIM) + RMS_EPS) * qg_ref[...]
    qt_ref[...] = _rope(q, cos, sin, first_half).T.astype(BF16)
    yield
    k = z_ref[:, OFF_K:OFF_K + D_KV]
    k = k * lax.rsqrt(_head_sumsq(k, obd) * (1.0 / HEAD_DIM) + RMS_EPS) * kg_ref[...]
    k_ref[...] = _rope(k, cos, sin, first_half).astype(BF16)
    vt = z_ref[:, OFF_V:OFF_V + D_KV].T.astype(BF16)
    ones = jnp.ones((HEAD_DIM, TOK_TILE), BF16)
    va_ref[0, 0] = jnp.concatenate([vt[0:HEAD_DIM], ones], axis=0)
    va_ref[1, 0] = jnp.concatenate([ones, vt[HEAD_DIM:2 * HEAD_DIM]], axis=0)
    yield
    gb_ref[...] = _silu(z_ref[:, OFF_BG:OFF_BG + D_ATTN]).astype(BF16)
    yield

    rx_ref[...] = z_ref[:, OFF_RX:OFF_RX + D_LRU]
    gr_ref[...] = _silu(z_ref[:, OFF_RG:OFF_RG + D_LRU]).astype(BF16)


def _stream_tile(x_ref, ctx_ref, is_ctx):
    if ctx_ref is None:
        return x_ref[...]
    return jnp.where(is_ctx, ctx_ref[...], x_ref[...])


def _in_kernel(*refs, nt, split):
    if split:
        x_ref, ctx_ref, mod_ref, w_ref, *rest = refs
    else:
        x_ref, mod_ref, w_ref, *rest = refs
        ctx_ref = None
    *epi_refs, z0_scr, z1_scr = rest
    i = pl.program_id(0)
    n_tile = pl.num_programs(0) - 1

    @pl.when(i == 0)
    def _():
        z1_scr[...] = jnp.zeros(z1_scr.shape, F32)

    def project(z_new):
        mod = mod_ref[...]
        shift = mod[:, 0:D_MODEL]
        scale = mod[:, D_MODEL:2 * D_MODEL]
        xt = _stream_tile(x_ref, ctx_ref, jnp.minimum(i, n_tile - 1) % nt == 0)
        xm = (xt * (1.0 + scale) + shift).astype(BF16)
        for c in range(D_IN // PROJ_COLS):
            cs = slice(c * PROJ_COLS, (c + 1) * PROJ_COLS)
            z_new[:, cs] = jnp.dot(xm, w_ref[:, cs], preferred_element_type=F32)
            yield

    def step(z_new, z_prev):
        stages = [project(z_new), _in_epilogue(z_prev, *epi_refs)]
        while stages:
            for st in list(stages):
                if next(st, StopIteration) is StopIteration:
                    stages.remove(st)

    @pl.when(i % 2 == 0)
    def _():
        step(z0_scr, z1_scr)

    @pl.when(i % 2 == 1)
    def _():
        step(z1_scr, z0_scr)


def _in_call(stream, ss, w_in, ang, anb, wcat, bs_full, qg, kg, cos_t, sin_t, obd):
    split = isinstance(stream, tuple)
    bn, lt = stream[0].shape[:2] if split else stream.shape[:2]
    lt += CTX_LEN if split else 0
    nt = lt // TOK_TILE
    n_tile = bn * nt
    cur = lambda i: jnp.minimum(i, n_tile - 1)
    prev = lambda i: jnp.maximum(i - 1, 0)
    tok = lambda width: pl.BlockSpec((None, TOK_TILE, width), lambda i: (prev(i) // nt, prev(i) % nt, 0))
    full = lambda a: pl.BlockSpec(a.shape, lambda i: (0,) * a.ndim)
    if split:
        stream_specs = [
            pl.BlockSpec((None, TOK_TILE, D_MODEL), lambda i: (cur(i) // nt, jnp.maximum(cur(i) % nt - 1, 0), 0)),
            pl.BlockSpec((None, CTX_LEN, D_MODEL), lambda i: (cur(i) // nt, 0, 0)),
        ]
    else:
        stream = (stream,)
        stream_specs = [pl.BlockSpec((None, TOK_TILE, D_MODEL), lambda i: (cur(i) // nt, cur(i) % nt, 0))]
    return pl.pallas_call(
        functools.partial(_in_kernel, nt=nt, split=split),
        grid=(n_tile + 1,),
        in_specs=stream_specs + [
            pl.BlockSpec((None, None, 1, 3 * D_MODEL),
                         lambda i: (cur(i) // nt, jnp.minimum(cur(i) % nt, 1), 0, 0)),
            full(w_in), full(ang), full(anb), full(wcat), full(bs_full), full(qg), full(kg),
            pl.BlockSpec((TOK_TILE, LANES), lambda i: (prev(i) % nt, 0)),
            pl.BlockSpec((TOK_TILE, LANES), lambda i: (prev(i) % nt, 0)),
            full(obd),
        ],
        out_specs=[
            tok(D_CHUNK),
            pl.BlockSpec((None, D_ATTN, TOK_TILE), lambda i: (prev(i) // nt, 0, prev(i) % nt)),
            tok(D_KV),
            pl.BlockSpec((None, N_KV_HEADS, 1, D_KV, TOK_TILE),
                         lambda i: (prev(i) // nt, 0, prev(i) % nt, 0, 0)),
            tok(D_ATTN), tok(D_LRU), tok(D_LRU),
        ],
        out_shape=[
            jax.ShapeDtypeStruct((bn, lt, D_CHUNK), BF16),
            jax.ShapeDtypeStruct((bn, D_ATTN, lt), BF16),
            jax.ShapeDtypeStruct((bn, lt, D_KV), BF16),
            jax.ShapeDtypeStruct((bn, N_KV_HEADS, nt, D_KV, TOK_TILE), BF16),
            jax.ShapeDtypeStruct((bn, lt, D_ATTN), BF16),
            jax.ShapeDtypeStruct((bn, lt, D_LRU), F32),
            jax.ShapeDtypeStruct((bn, lt, D_LRU), BF16),
        ],
        scratch_shapes=[pltpu.VMEM((TOK_TILE, D_IN), F32), pltpu.VMEM((TOK_TILE, D_IN), F32)],
        compiler_params=pltpu.CompilerParams(
            dimension_semantics=("arbitrary",), vmem_limit_bytes=VMEM_LIMIT),
        name="in_proj",
    )(*stream, ss, w_in, ang, anb, wcat, bs_full, qg, kg, cos_t, sin_t, obd)


def _sigmoid(t):
    return 0.5 * jnp.tanh(0.5 * t) + 0.5


def _lru_kernel(rx_ref, gr_ref, cw_ref, cb_ref, wg_ref, bg_ref, lam_ref, o_ref,
                xp_scr, a_scr, b_scr, p_scr, h_scr):
    lt = rx_ref.shape[0]
    n_lat = lt - CTX_LEN
    n_chunk = lt // LRU_T
    n_ctx_chunk = CTX_LEN // LRU_T
    ctx_rows = LRU_SEG * CTX_PITCH
    lat_rows = LRU_SEG * LAT_PITCH
    assert CTX_LEN <= ctx_rows and n_lat <= lat_rows and (ctx_rows - CTX_LEN) % SUBLANES == 0
    regions = ((0, CTX_PITCH, CTX_LEN, 0), (ctx_rows, LAT_PITCH, n_lat, CTX_LEN))
    slabs = D_LRU // LANES

    pad = SUBLANES
    zpad = jnp.zeros((pad, D_LRU), F32)
    xp_scr[0:pad] = zpad
    xp_scr[pad + CTX_LEN:2 * pad + CTX_LEN] = zpad
    xp_scr[2 * pad + lt:3 * pad + lt] = zpad
    xp_scr[pad:pad + CTX_LEN] = rx_ref[0:CTX_LEN]
    xp_scr[2 * pad + CTX_LEN:2 * pad + lt] = rx_ref[CTX_LEN:lt]
    for off, pitch, rows, _ in regions:
        n_pad = LRU_SEG * pitch - rows
        for d in range(N_DIR):
            for sl in range(slabs):
                a_scr[d, sl, off + rows:off + rows + n_pad, :] = jnp.ones((n_pad, LANES), F32)
                b_scr[d, sl, off + rows:off + rows + n_pad, :] = jnp.zeros((n_pad, LANES), F32)

    cw = cw_ref[...]
    cb = cb_ref[...]
    win = LRU_T + 2 * pad
    half_c_sp = []
    for d in range(N_DIR):
        nl = -lam_ref[d:d + 1, :]
        softplus = jnp.maximum(nl, 0.0) + jnp.log1p(jnp.exp(-jnp.abs(nl)))
        half_c_sp.append((0.5 * LRU_C) * softplus)

    def coeff_body(i, carry):
        base = pl.multiple_of(i * LRU_T, LRU_T)
        in_ctx = i < n_ctx_chunk
        poff = pl.multiple_of(base + jnp.where(in_ctx, 0, pad), pad)
        w = xp_scr[pl.ds(poff, win), :]
        xr = cb + cw[2:3] * w[pad:pad + LRU_T]
        xr = xr + cw[0:1] * pltpu.roll(w, 2, 0)[pad:pad + LRU_T]
        xr = xr + cw[1:2] * pltpu.roll(w, 1, 0)[pad:pad + LRU_T]
        xr = xr + cw[3:4] * pltpu.roll(w, win - 1, 0)[pad:pad + LRU_T]
        g = jnp.dot(xr.astype(BF16), wg_ref[...], preferred_element_type=F32) + bg_ref[...]
        srow = pl.multiple_of(base + jnp.where(in_ctx, 0, ctx_rows - CTX_LEN), SUBLANES)
        half_x = 0.5 * xr
        for d in range(N_DIR):
            t_r = jnp.tanh(0.5 * g[:, 2 * d * D_LRU:(2 * d + 1) * D_LRU])
            t_i = jnp.tanh(0.5 * g[:, (2 * d + 1) * D_LRU:(2 * d + 2) * D_LRU])
            neg_log_a = half_c_sp[d] * t_r + half_c_sp[d]
            a = jnp.exp2(neg_log_a * (-LOG2E))
            y = jnp.tanh(neg_log_a) * (1.0 + a * a)
            root = jnp.where(y > 0.0, y * lax.rsqrt(y), 0.0)
            b = root * (half_x * t_i + half_x)
            for sl in range(slabs):
                a_scr[d, sl, pl.ds(srow, LRU_T), :] = a[:, sl * LANES:(sl + 1) * LANES]
                b_scr[d, sl, pl.ds(srow, LRU_T), :] = b[:, sl * LANES:(sl + 1) * LANES]
        return carry

    lax.fori_loop(0, n_chunk, coeff_body, 0)

    def sweep(off, pitch):
        def body(i, state):
            new = []
            for d in range(N_DIR):
                t = i if d == 0 else pitch - 1 - i
                idx = pl.ds(off + t, LRU_SEG, stride=pitch)
                for sl in range(slabs):
                    h, p = state[2 * (d * slabs + sl)], state[2 * (d * slabs + sl) + 1]
                    a = a_scr[d, sl, idx, :]
                    h = a * h + b_scr[d, sl, idx, :]
                    p = a * p
                    h_scr[d, sl, idx, :] = h
                    p_scr[d, sl, idx, :] = p
                    new += [h, p]
            return tuple(new)

        ident = (jnp.zeros((LRU_SEG, LANES), F32), jnp.ones((LRU_SEG, LANES), F32)) * (N_DIR * slabs)
        lax.fori_loop(0, pitch, body, ident, unroll=4)

    for off, pitch, _, _ in regions:
        sweep(off, pitch)

    def chain(d, off, pitch, init):
        carries = [None] * LRU_SEG
        c = init
        for j in (range(LRU_SEG) if d == 0 else reversed(range(LRU_SEG))):
            carries[j] = c
            last = off + j * pitch + (pitch - 1 if d == 0 else 0)
            c = [h_scr[d, sl, last:last + 1, :] + p_scr[d, sl, last:last + 1, :] * c[sl]
                 for sl in range(slabs)]
        return carries, c

    zero = [jnp.zeros((1, LANES), F32)] * slabs
    carries = []
    for d in range(N_DIR):
        c_ctx, fin = chain(d, regions[0][0], regions[0][1], zero)
        c_lat, _ = chain(d, regions[1][0], regions[1][1], fin)
        carries.append((c_ctx, c_lat))

    row = lax.broadcasted_iota(jnp.int32, (LRU_T, 1), 0)
    for ci in range(n_chunk):
        reg = 0 if ci < n_ctx_chunk else 1
        off, pitch, _, tok0 = regions[reg]
        r0 = ci * LRU_T - tok0
        j_lo, j_hi = r0 // pitch, (r0 + LRU_T - 1) // pitch
        for sl in range(slabs):
            rows = slice(off + r0, off + r0 + LRU_T)
            h = None
            for d in range(N_DIR):
                seg_c = carries[d][reg]
                c = seg_c[j_hi][sl]
                for j in range(j_hi - 1, j_lo - 1, -1):
                    c = jnp.where(row < (j + 1) * pitch - r0, seg_c[j][sl], c)
                hd = h_scr[d, sl, rows, :] + p_scr[d, sl, rows, :] * c
                h = hd if h is None else h + hd
            cols = slice(sl * LANES, (sl + 1) * LANES)
            gate = gr_ref[ci * LRU_T:(ci + 1) * LRU_T, cols].astype(F32)
            o_ref[ci * LRU_T:(ci + 1) * LRU_T, cols] = (h * gate).astype(BF16)


def _lru_call(rx, gr, cw, cb, wg, bg, lam):
    bn, lt, _ = rx.shape
    full = lambda a: pl.BlockSpec(a.shape, lambda b: (0,) * a.ndim)
    seq = pl.BlockSpec((None, lt, D_LRU), lambda b: (b, 0, 0))
    scan_rows = LRU_SEG * (CTX_PITCH + LAT_PITCH)
    return pl.pallas_call(
        _lru_kernel,
        grid=(bn,),
        in_specs=[seq, seq, full(cw), full(cb), full(wg), full(bg), full(lam)],
        out_specs=seq,
        out_shape=jax.ShapeDtypeStruct((bn, lt, D_LRU), BF16),
        scratch_shapes=[
            pltpu.VMEM((lt + 3 * SUBLANES, D_LRU), F32),
        ] + [pltpu.VMEM((N_DIR, D_LRU // LANES, scan_rows, LANES), F32)] * 4,
        compiler_params=pltpu.CompilerParams(
            dimension_semantics=("parallel",), vmem_limit_bytes=VMEM_LIMIT),
        name="rglru",
    )(rx, gr, cw, cb, wg, bg, lam)


def _attend(qt_ref, k_ref, va_ref, att_scr, rhs_scr, s_scr, p_scr, ot_scr, nk, bound):
    pair = LANES // HEAD_DIM
    width = GQA_GROUP * Q_TILE
    n_chunk = nk // KEY_CHUNK
    units = [(qb, g) for qb in range(qt_ref.shape[1] // Q_TILE) for g in range(N_KV_HEADS)]
    n_unit = len(units)
    zeros = jnp.zeros((HEAD_DIM, width), BF16)

    def stage_rhs(u):
        qb, g = units[u]
        h0 = g * GQA_GROUP
        qt = jnp.concatenate(
            [qt_ref[(h0 + j) * HEAD_DIM:(h0 + j + 1) * HEAD_DIM, qb * Q_TILE:(qb + 1) * Q_TILE]
             for j in range(GQA_GROUP)], axis=1)
        rhs_scr[u % 2] = jnp.concatenate([qt, zeros] if g == 0 else [zeros, qt], axis=0)

    def score_chunk(u, c, mx):
        ks = pl.ds(pl.multiple_of(c * KEY_CHUNK, KEY_CHUNK), KEY_CHUNK)
        s = jnp.dot(k_ref[ks, :], rhs_scr[u % 2], preferred_element_type=F32)
        s_scr[u % 2, c] = s
        for i in range(KEY_CHUNK // SUBLANES):
            mx = jnp.maximum(mx, s[i * SUBLANES:(i + 1) * SUBLANES])
        return mx

    def exp_chunk(u, c, m):
        p_scr[u % 2, c] = jnp.exp2(s_scr[u % 2, c] - m).astype(BF16)

    def pv_chunk(u, c):
        g = units[u][1]
        ot_scr[...] += jnp.dot(va_ref[g, c], p_scr[u % 2, c], preferred_element_type=F32)

    def finish(u):
        qb, g = units[u]
        ot = ot_scr[...]
        if g == 0:
            o, l = ot[0:HEAD_DIM], ot[HEAD_DIM:HEAD_DIM + 1]
        else:
            o, l = ot[HEAD_DIM:2 * HEAD_DIM], ot[0:1]
        o = o * (1.0 / l)
        for rp in range(GQA_GROUP // pair):
            stacked = jnp.concatenate(
                [o[:, (rp * pair + j) * Q_TILE:(rp * pair + j + 1) * Q_TILE] for j in range(pair)], axis=0)
            col = (g * GQA_GROUP + rp * pair) // pair
            att_scr[qb * Q_TILE:(qb + 1) * Q_TILE, col * LANES:(col + 1) * LANES] = stacked.T

    def score_exp_chunk(u, c):
        ks = pl.ds(pl.multiple_of(c * KEY_CHUNK, KEY_CHUNK), KEY_CHUNK)
        s = jnp.dot(k_ref[ks, :], rhs_scr[u % 2], preferred_element_type=F32)
        p_scr[u % 2, c] = jnp.exp2(s - bound).astype(BF16)

    if bound is not None:
        for t in range(n_unit + 1):
            ua, uc = t, t - 1
            do_a, do_c = ua < n_unit, uc >= 0
            if do_a:
                stage_rhs(ua)
            if do_c:
                ot_scr[...] = jnp.zeros(ot_scr.shape, F32)

            def fast_body(c, carry, ua=ua, uc=uc, do_a=do_a, do_c=do_c):
                if do_a:
                    score_exp_chunk(ua, c)
                if do_c:
                    pv_chunk(uc, c)
                return carry

            lax.fori_loop(0, n_chunk, fast_body, 0, unroll=min(PIPE_UNROLL, n_chunk))
            if do_c:
                finish(uc)
        return

    col_max = {}
    for t in range(n_unit + 2):
        ua, ub, uc = t, t - 1, t - 2
        do_a, do_b, do_c = ua < n_unit, 0 <= ub < n_unit, 0 <= uc < n_unit
        if do_a:
            stage_rhs(ua)
        if do_c:
            ot_scr[...] = jnp.zeros(ot_scr.shape, F32)

        def body(c, mx, ua=ua, ub=ub, uc=uc, do_a=do_a, do_b=do_b, do_c=do_c):
            if do_a:
                mx = score_chunk(ua, c, mx)
            if do_b:
                exp_chunk(ub, c, col_max[ub])
            if do_c:
                pv_chunk(uc, c)
            return mx

        mx = lax.fori_loop(0, n_chunk, body, jnp.full((SUBLANES, width), -jnp.inf, F32),
                           unroll=min(PIPE_UNROLL, n_chunk))
        if do_a:
            col_max[ua] = jnp.max(mx, axis=0, keepdims=True)
        if do_c:
            finish(uc)


def _attn_kernel(flag_ref, bnd_ref, qt_ref, k_ref, va_ref, oa_ref, gb_ref, ol_ref, *refs,
                 q_off, with_ctx, split):
    if split:
        x_ref, ctx_ref, *refs = refs
    else:
        x_ref, *refs = refs
        ctx_ref = None
    mod_ref, wo_ref, lng_ref, lnb_ref, o_ref, att_scr, rhs_scr, s_scr, p_scr, ot_scr = refs
    nk_all = k_ref.shape[0]
    scratch = (att_scr, rhs_scr, s_scr, p_scr, ot_scr)
    bounded = flag_ref[0] == 1
    is_ctx = (pl.program_id(1) + q_off) < (CTX_LEN // Q_STEP)
    for ctx_step, nk in ((True, CTX_LEN), (False, nk_all)):
        if ctx_step and not with_ctx:
            continue
        here = (is_ctx if ctx_step else jnp.logical_not(is_ctx)) if with_ctx else True

        @pl.when(jnp.logical_and(here, bounded))
        def _():
            _attend(qt_ref, k_ref, va_ref, *scratch, nk, bnd_ref[...])

        @pl.when(jnp.logical_and(here, jnp.logical_not(bounded)))
        def _():
            _attend(qt_ref, k_ref, va_ref, *scratch, nk, None)

    att = (att_scr[...] * gb_ref[...].astype(F32)).astype(BF16)
    cat = jnp.concatenate([oa_ref[...], att, ol_ref[...]], axis=1)
    y = jnp.dot(cat, wo_ref[...], preferred_element_type=F32)
    gate = mod_ref[:, 2 * D_MODEL:3 * D_MODEL]
    xt = _stream_tile(x_ref, ctx_ref, is_ctx)
    o_ref[...] = _layer_norm(ALPHA * xt + gate * y, lng_ref[...], lnb_ref[...])


def _attn_call(flag, bnd, qt, k, va, oa, gb, ol, stream, ss, w_o, lng, lnb, last):
    split = isinstance(stream, tuple)
    assert not (split and last)
    bn, lt, _ = oa.shape
    n_ctx_step = CTX_LEN // Q_STEP
    q_off = n_ctx_step if last else 0
    nq = lt // Q_STEP - q_off
    tok = lambda width: pl.BlockSpec((None, Q_STEP, width), lambda b, i: (b, i + q_off, 0))
    full = lambda a: pl.BlockSpec(a.shape, lambda b, i: (0,) * a.ndim)
    if split:
        stream_specs = [
            pl.BlockSpec((None, Q_STEP, D_MODEL), lambda b, i: (b, jnp.maximum(i - 1, 0), 0)),
            pl.BlockSpec((None, CTX_LEN, D_MODEL), lambda b, i: (b, 0, 0)),
        ]
    else:
        stream = (stream,)
        stream_specs = [tok(D_MODEL)]
    return pl.pallas_call(
        functools.partial(_attn_kernel, q_off=q_off, with_ctx=not last, split=split),
        grid=(bn, nq),
        in_specs=[
            pl.BlockSpec(memory_space=pltpu.SMEM),
            full(bnd),
            pl.BlockSpec((None, D_ATTN, Q_STEP), lambda b, i: (b, 0, i + q_off)),
            pl.BlockSpec((None, lt, D_KV), lambda b, i: (b, 0, 0)),
            pl.BlockSpec((None, N_KV_HEADS, lt // KEY_CHUNK, D_KV, KEY_CHUNK), lambda b, i: (b, 0, 0, 0, 0)),
            tok(D_CHUNK), tok(D_ATTN), tok(D_LRU),
        ] + stream_specs + [
            pl.BlockSpec((None, None, 1, 3 * D_MODEL),
                         lambda b, i: (b, jnp.minimum((i + q_off) // n_ctx_step, 1), 0, 0)),
            full(w_o), full(lng), full(lnb),
        ],
        out_specs=pl.BlockSpec((None, Q_STEP, D_MODEL), lambda b, i: (b, i, 0)),
        out_shape=jax.ShapeDtypeStruct((bn, nq * Q_STEP, D_MODEL), F32),
        scratch_shapes=[
            pltpu.VMEM((Q_STEP, D_ATTN), F32),
            pltpu.VMEM((2, D_KV, GQA_GROUP * Q_TILE), BF16),
            pltpu.VMEM((2, lt // KEY_CHUNK, KEY_CHUNK, GQA_GROUP * Q_TILE), F32),
            pltpu.VMEM((2, lt // KEY_CHUNK, KEY_CHUNK, GQA_GROUP * Q_TILE), BF16),
            pltpu.VMEM((D_KV, GQA_GROUP * Q_TILE), F32),
        ],
        compiler_params=pltpu.CompilerParams(
            dimension_semantics=("parallel", "arbitrary"), vmem_limit_bytes=VMEM_LIMIT),
        name="attn_merge",
    )(flag, bnd, qt, k, va, oa, gb, ol, *stream, ss, w_o, lng, lnb)


def _rope_tables(n_lat):
    nf = HEAD_DIM // 4
    t = jnp.arange(n_lat, dtype=jnp.int32)
    pos = jnp.stack([t // GRID_W, t % GRID_W], axis=1).astype(F32)
    inv = ROPE_THETA ** (-jnp.arange(nf, dtype=F32) / nf)
    d = np.arange(HEAD_DIM)
    ang = pos[:, d // (HEAD_DIM // 2)] * inv[d % nf]
    sign = jnp.asarray(np.where((d % (HEAD_DIM // 2)) < nf, -1.0, 1.0), F32)
    cos = jnp.concatenate([jnp.ones((CTX_LEN, HEAD_DIM), F32), jnp.cos(ang)], axis=0)
    sin = jnp.concatenate([jnp.zeros((CTX_LEN, HEAD_DIM), F32), jnp.sin(ang) * sign], axis=0)
    reps = LANES // HEAD_DIM
    return jnp.tile(cos, (1, reps)), jnp.tile(sin, (1, reps))


def _block_diag(w):
    nb, n, _ = w.shape
    eye = jnp.eye(nb, dtype=w.dtype)
    return (eye[:, None, :, None] * w[:, :, None, :]).reshape(nb * n, nb * n)


def kernel(x, c, ctx, c_ctx, w_ada, b_ada, w_in, a_norm_g, a_norm_b, a_ws, a_bs, q_norm_g, k_norm_g,
           conv_w, conv_b, lru_wr, lru_br, lru_wi, lru_bi, lru_lam, w_o, ln_g, ln_b):
    bn, n_lat, _ = x.shape
    assert ctx.shape[1] == CTX_LEN and n_lat % TOK_TILE == 0 and bn + 1 <= MOD_ROWS
    xa = (x, ctx)

    cc = jnp.zeros((MOD_ROWS, D_MODEL), F32).at[:bn].set(c).at[bn].set(c_ctx)
    mod = _ada_call(cc, w_ada, b_ada)
    ctx_mod = jnp.broadcast_to(mod[:, bn:bn + 1], (DEPTH, bn, 3 * D_MODEL))
    ss = jnp.stack([ctx_mod, mod[:, :bn]], axis=2).reshape(DEPTH, bn, 2, 1, 3 * D_MODEL)

    cos_t, sin_t = _rope_tables(n_lat)
    head = np.arange(LANES) // HEAD_DIM
    obd = jnp.asarray(head[:, None] == head[None, :], BF16)

    w_in_b = w_in.astype(BF16)
    w_o_b = w_o.astype(BF16)
    for l in range(DEPTH):
        wcat = jnp.concatenate([a_ws[l, g] for g in range(A_GROUPS)], axis=1).astype(BF16)
        bs_full = jnp.repeat(a_bs[l].T, A_GDIM, axis=1)
        qg = (jnp.tile(q_norm_g[l], N_HEADS) * (HEAD_DIM ** -0.5 * LOG2E)).reshape(1, D_ATTN)
        kg = jnp.tile(k_norm_g[l], N_KV_HEADS).reshape(1, D_KV)
        oa, qt, k, va, gb, rx, gr = _in_call(
            xa, ss[l], w_in_b[l], a_norm_g[l].reshape(1, D_CHUNK), a_norm_b[l].reshape(1, D_CHUNK),
            wcat, bs_full, qg, kg, cos_t, sin_t, obd)

        wg = jnp.concatenate(
            [m for d in range(N_DIR) for m in (_block_diag(lru_wr[l, d]), _block_diag(lru_wi[l, d]))],
            axis=1).astype(BF16)
        bg = jnp.concatenate(
            [m for d in range(N_DIR) for m in (lru_br[l, d], lru_bi[l, d])]).reshape(1, 4 * D_LRU)
        ol = _lru_call(rx, gr, conv_w[l], conv_b[l].reshape(1, D_LRU), wg, bg, lru_lam[l])

        s_bound = (SCORE_BOUND_SLACK * HEAD_DIM ** 0.5 * LOG2E
                   * jnp.max(jnp.abs(q_norm_g[l])) * jnp.max(jnp.abs(k_norm_g[l])))
        flag = (s_bound <= MAX_SAFE_SCORE_BOUND).astype(jnp.int32).reshape(1)
        bnd = jnp.full((1, GQA_GROUP * Q_TILE), s_bound, F32)
        xa = _attn_call(flag, bnd, qt, k, va, oa, gb, ol, xa, ss[l], w_o_b[l], ln_g[l].reshape(1, D_MODEL),
                        ln_b[l].reshape(1, D_MODEL), last=(l == DEPTH - 1))
    return xa
```

```python
import functools

import jax
import jax.numpy as jnp
import numpy as np
from jax import lax
from jax.experimental import pallas as pl
from jax.experimental.pallas import tpu as pltpu

F32 = jnp.float32
BF16 = jnp.bfloat16

D_MODEL = 1024
DEPTH = 4
GRID_W = 64
CTX_LEN = 256
D_CHUNK = D_MODEL // 4
D_ATTN = D_MODEL // 2
D_LRU = D_MODEL // 4
CHUNK = 128
A_GROUPS = 4
A_GDIM = D_CHUNK // A_GROUPS
HEAD_DIM = 64
N_HEADS = D_ATTN // HEAD_DIM
N_KV_HEADS = N_HEADS // 4
GQA_GROUP = N_HEADS // N_KV_HEADS
D_KV = N_KV_HEADS * HEAD_DIM
ROPE_THETA = 10000.0
LRU_BLOCKS = 4
LRU_BDIM = D_LRU // LRU_BLOCKS
CONV_W = 4
LRU_C = 8.0
N_DIR = 2
D_IN = 3 * D_CHUNK + 2 * D_ATTN + 2 * D_KV + 2 * D_LRU
ALPHA = (2.0 * DEPTH) ** 0.25
LN_EPS = 1e-6
RMS_EPS = 1e-6
LOG2E = 1.4426950408889634
SCORE_BOUND_SLACK = 1.02
MAX_SAFE_SCORE_BOUND = 40.0

OFF_AU, OFF_AV, OFF_AG = 0, D_CHUNK, 2 * D_CHUNK
OFF_Q = 3 * D_CHUNK
OFF_K = OFF_Q + D_ATTN
OFF_V = OFF_K + D_KV
OFF_BG = OFF_V + D_KV
OFF_RX = OFF_BG + D_ATTN
OFF_RG = OFF_RX + D_LRU

LANES = 128
SUBLANES = 8
TOK_TILE = 256
PROJ_COLS = 512
Q_TILE = 128
Q_STEP = 256
PIPE_UNROLL = 9
KEY_CHUNK = 256
LRU_T = 128
LRU_SEG = 8
CTX_PITCH = 36
LAT_PITCH = 260
MOD_ROWS = 24
VMEM_LIMIT = 48 * 1024 * 1024
LRU_VMEM_LIMIT = 56 * 1024 * 1024


def _layer_norm(t, g, b):
    mu = jnp.mean(t, axis=-1, keepdims=True)
    d = t - mu
    var = jnp.mean(d * d, axis=-1, keepdims=True)
    return d * lax.rsqrt(var + LN_EPS) * g + b


def _ada_kernel(c_ref, w_ref, b_ref, o_ref):
    h = jax.nn.silu(c_ref[...]).astype(BF16)
    o_ref[...] = jnp.dot(h, w_ref[...].astype(BF16), preferred_element_type=F32) + b_ref[...]


def _ada_call(cc, w_ada, b_ada):
    n_col = 3 * D_MODEL // D_MODEL
    return pl.pallas_call(
        _ada_kernel,
        grid=(DEPTH, n_col),
        in_specs=[
            pl.BlockSpec((MOD_ROWS, D_MODEL), lambda l, j: (0, 0)),
            pl.BlockSpec((None, D_MODEL, D_MODEL), lambda l, j: (l, 0, j)),
            pl.BlockSpec((None, 1, D_MODEL), lambda l, j: (l, 0, j)),
        ],
        out_specs=pl.BlockSpec((None, MOD_ROWS, D_MODEL), lambda l, j: (l, 0, j)),
        out_shape=jax.ShapeDtypeStruct((DEPTH, MOD_ROWS, 3 * D_MODEL), F32),
        compiler_params=pltpu.CompilerParams(
            dimension_semantics=("arbitrary", "arbitrary"), vmem_limit_bytes=VMEM_LIMIT),
        name="adaln",
    )(cc, w_ada, b_ada.reshape(DEPTH, 1, 3 * D_MODEL))


def _head_sumsq(t, obd):
    sq = t * t
    hi = sq.astype(BF16)
    lo = (sq - hi.astype(F32)).astype(BF16)
    cols = []
    for c in range(t.shape[1] // LANES):
        sl = slice(c * LANES, (c + 1) * LANES)
        cols.append(jnp.dot(jnp.concatenate([hi[:, sl], lo[:, sl]], axis=1), obd,
                            preferred_element_type=F32))
    return cols[0] if len(cols) == 1 else jnp.concatenate(cols, axis=1)


def _rope(t, cos, sin, first_half):
    cols = []
    for c in range(t.shape[1] // LANES):
        xc = t[:, c * LANES:(c + 1) * LANES]
        up = pltpu.roll(xc, LANES - HEAD_DIM // 4, 1)
        dn = pltpu.roll(xc, HEAD_DIM // 4, 1)
        cols.append(xc * cos + jnp.where(first_half, up, dn) * sin)
    return cols[0] if len(cols) == 1 else jnp.concatenate(cols, axis=1)


def _silu(t):
    return t * _sigmoid(t)


def _lru_coefficients(w, cw_ref, cb_ref, wg_ref, bg_ref, lam_ref):
    pad = SUBLANES
    win = w.shape[0]
    n = win - 2 * pad
    cw = cw_ref[...]
    xr = cb_ref[...] + cw[2:3] * w[pad:pad + n]
    xr = xr + cw[0:1] * pltpu.roll(w, 2, 0)[pad:pad + n]
    xr = xr + cw[1:2] * pltpu.roll(w, 1, 0)[pad:pad + n]
    xr = xr + cw[3:4] * pltpu.roll(w, win - 1, 0)[pad:pad + n]
    g = jnp.dot(xr.astype(BF16), wg_ref[...], preferred_element_type=F32) + bg_ref[...]
    half_x = 0.5 * xr
    out = []
    for d in range(N_DIR):
        nl = -lam_ref[d:d + 1, :]
        softplus = jnp.maximum(nl, 0.0) + jnp.log1p(jnp.exp(-jnp.abs(nl)))
        half_c_sp = (0.5 * LRU_C) * softplus
        t_r = jnp.tanh(0.5 * g[:, 2 * d * D_LRU:(2 * d + 1) * D_LRU])
        t_i = jnp.tanh(0.5 * g[:, (2 * d + 1) * D_LRU:(2 * d + 2) * D_LRU])
        neg_log_a = half_c_sp * t_r + half_c_sp
        a = jnp.exp2(neg_log_a * (-LOG2E))
        y = jnp.tanh(neg_log_a) * (1.0 + a * a)
        root = jnp.where(y > 0.0, y * lax.rsqrt(y), 0.0)
        out.append((a, root * (half_x * t_i + half_x)))
    return out


def _in_epilogue(z_ref, z_next_ref, seg_start, seg_end, carry_scr,
                 ang_ref, anb_ref, wcat_ref, bs_ref, qg_ref, kg_ref, cos_ref, sin_ref, obd_ref,
                 cw_ref, cb_ref, wg_ref, bg_ref, lam_ref,
                 oa_ref, qt_ref, k_ref, va_ref, gb_ref, gr_ref, a_ref, b_ref):
    rx = z_ref[:, OFF_RX:OFF_RX + D_LRU]
    head = jnp.where(seg_start, 0.0, carry_scr[...])
    tail = jnp.where(seg_end, 0.0, z_next_ref[0:SUBLANES, OFF_RX:OFF_RX + D_LRU])
    carry_scr[...] = rx[TOK_TILE - SUBLANES:]
    coeff = _lru_coefficients(jnp.concatenate([head, rx, tail], axis=0),
                              cw_ref, cb_ref, wg_ref, bg_ref, lam_ref)
    for d, (a, b) in enumerate(coeff):
        for sl in range(D_LRU // LANES):
            a_ref[d, sl] = a[:, sl * LANES:(sl + 1) * LANES]
            b_ref[d, sl] = b[:, sl * LANES:(sl + 1) * LANES]
    yield

    u = jax.nn.gelu(z_ref[:, OFF_AU:OFF_AU + D_CHUNK])
    vn = _layer_norm(jax.nn.gelu(z_ref[:, OFF_AV:OFF_AV + D_CHUNK]), ang_ref[...], anb_ref[...])
    yield
    ga = _silu(z_ref[:, OFF_AG:OFF_AG + D_CHUNK])
    grp = lax.broadcasted_iota(jnp.int32, (1, D_CHUNK), 1) // A_GDIM
    for ch in range(TOK_TILE // CHUNK):
        rs = slice(ch * CHUNK, (ch + 1) * CHUNK)
        vc = vn[rs]
        vstack = jnp.concatenate(
            [jnp.where(grp == g, vc, 0.0) for g in range(A_GROUPS)], axis=0).astype(BF16)
        s = jnp.dot(wcat_ref[...], vstack, preferred_element_type=F32) + bs_ref[...]
        oa_ref[rs, :] = (u[rs] * s * ga[rs]).astype(BF16)
    yield

    obd = obd_ref[...]
    cos = cos_ref[...]
    sin = sin_ref[...]
    lane = lax.broadcasted_iota(jnp.int32, (1, LANES), 1)
    first_half = (lane % (HEAD_DIM // 2)) < (HEAD_DIM // 4)
    q = z_ref[:, OFF_Q:OFF_Q + D_ATTN]
    q = q * lax.rsqrt(_head_sumsq(q, obd) * (1.0 / HEAD_DIM) + RMS_EPS) * qg_ref[...]
    qt_ref[...] = _rope(q, cos, sin, first_half).T.astype(BF16)
    yield
    k = z_ref[:, OFF_K:OFF_K + D_KV]
    k = k * lax.rsqrt(_head_sumsq(k, obd) * (1.0 / HEAD_DIM) + RMS_EPS) * kg_ref[...]
    k_ref[...] = _rope(k, cos, sin, first_half).astype(BF16)
    vt = z_ref[:, OFF_V:OFF_V + D_KV].T.astype(BF16)
    ones = jnp.ones((HEAD_DIM, TOK_TILE), BF16)
    va_ref[0, 0] = jnp.concatenate([vt[0:HEAD_DIM], ones], axis=0)
    va_ref[1, 0] = jnp.concatenate([ones, vt[HEAD_DIM:2 * HEAD_DIM]], axis=0)
    yield
    gb_ref[...] = _silu(z_ref[:, OFF_BG:OFF_BG + D_ATTN]).astype(BF16)
    yield

    gr_ref[...] = _silu(z_ref[:, OFF_RG:OFF_RG + D_LRU]).astype(BF16)


def _stream_tile(x_ref, ctx_ref, is_ctx, rows=slice(None)):
    if ctx_ref is None:
        return x_ref[rows, :]
    return jnp.where(is_ctx, ctx_ref[rows, :], x_ref[rows, :])


def _in_kernel(*refs, nt, split):
    if split:
        x_ref, ctx_ref, mod_ref, w_ref, *rest = refs
    else:
        x_ref, mod_ref, w_ref, *rest = refs
        ctx_ref = None
    *epi_refs, z0_scr, z1_scr, carry_scr = rest
    i = pl.program_id(0)
    n_tile = pl.num_programs(0) - 1
    t_prev = jnp.maximum(i - 1, 0) % nt
    seg_start = jnp.logical_or(t_prev == 0, t_prev == CTX_LEN // TOK_TILE)
    seg_end = jnp.logical_or(t_prev == CTX_LEN // TOK_TILE - 1, t_prev == nt - 1)

    @pl.when(i == 0)
    def _():
        z1_scr[...] = jnp.zeros(z1_scr.shape, F32)
        carry_scr[...] = jnp.zeros(carry_scr.shape, F32)

    def project(z_new):
        mod = mod_ref[...]
        shift = mod[:, 0:D_MODEL]
        scale = mod[:, D_MODEL:2 * D_MODEL]
        xt = _stream_tile(x_ref, ctx_ref, jnp.minimum(i, n_tile - 1) % nt == 0)
        xm = (xt * (1.0 + scale) + shift).astype(BF16)
        n_col = D_IN // PROJ_COLS
        for c in [OFF_RX // PROJ_COLS] + [c for c in range(n_col) if c != OFF_RX // PROJ_COLS]:
            cs = slice(c * PROJ_COLS, (c + 1) * PROJ_COLS)
            z_new[:, cs] = jnp.dot(xm, w_ref[:, cs], preferred_element_type=F32)
            yield

    def step(z_new, z_prev):
        stages = [project(z_new), _in_epilogue(z_prev, z_new, seg_start, seg_end, carry_scr, *epi_refs)]
        while stages:
            for st in list(stages):
                if next(st, StopIteration) is StopIteration:
                    stages.remove(st)

    @pl.when(i % 2 == 0)
    def _():
        step(z0_scr, z1_scr)

    @pl.when(i % 2 == 1)
    def _():
        step(z1_scr, z0_scr)


def _in_call(stream, ss, w_in, ang, anb, wcat, bs_full, qg, kg, cos_t, sin_t, obd, cw, cb, wg, bg, lam):
    split = isinstance(stream, tuple)
    bn, lt = stream[0].shape[:2] if split else stream.shape[:2]
    lt += CTX_LEN if split else 0
    nt = lt // TOK_TILE
    n_tile = bn * nt
    cur = lambda i: jnp.minimum(i, n_tile - 1)
    prev = lambda i: jnp.maximum(i - 1, 0)
    tok = lambda width: pl.BlockSpec((None, TOK_TILE, width), lambda i: (prev(i) // nt, prev(i) % nt, 0))
    full = lambda a: pl.BlockSpec(a.shape, lambda i: (0,) * a.ndim)
    slabs = D_LRU // LANES
    coeff_spec = pl.BlockSpec((None, N_DIR, slabs, TOK_TILE, LANES), lambda i: (prev(i) // nt, 0, 0, prev(i) % nt, 0))
    coeff_shape = jax.ShapeDtypeStruct((bn, N_DIR, slabs, lt, LANES), F32)
    if split:
        stream_specs = [
            pl.BlockSpec((None, TOK_TILE, D_MODEL), lambda i: (cur(i) // nt, jnp.maximum(cur(i) % nt - 1, 0), 0)),
            pl.BlockSpec((None, CTX_LEN, D_MODEL), lambda i: (cur(i) // nt, 0, 0)),
        ]
    else:
        stream = (stream,)
        stream_specs = [pl.BlockSpec((None, TOK_TILE, D_MODEL), lambda i: (cur(i) // nt, cur(i) % nt, 0))]
    return pl.pallas_call(
        functools.partial(_in_kernel, nt=nt, split=split),
        grid=(n_tile + 1,),
        in_specs=stream_specs + [
            pl.BlockSpec((None, None, 1, 3 * D_MODEL),
                         lambda i: (cur(i) // nt, jnp.minimum(cur(i) % nt, 1), 0, 0)),
            full(w_in), full(ang), full(anb), full(wcat), full(bs_full), full(qg), full(kg),
            pl.BlockSpec((TOK_TILE, LANES), lambda i: (prev(i) % nt, 0)),
            pl.BlockSpec((TOK_TILE, LANES), lambda i: (prev(i) % nt, 0)),
            full(obd), full(cw), full(cb), full(wg), full(bg), full(lam),
        ],
        out_specs=[
            tok(D_CHUNK),
            pl.BlockSpec((None, D_ATTN, TOK_TILE), lambda i: (prev(i) // nt, 0, prev(i) % nt)),
            tok(D_KV),
            pl.BlockSpec((None, N_KV_HEADS, 1, D_KV, TOK_TILE),
                         lambda i: (prev(i) // nt, 0, prev(i) % nt, 0, 0)),
            tok(D_ATTN), tok(D_LRU), coeff_spec, coeff_spec,
        ],
        out_shape=[
            jax.ShapeDtypeStruct((bn, lt, D_CHUNK), BF16),
            jax.ShapeDtypeStruct((bn, D_ATTN, lt), BF16),
            jax.ShapeDtypeStruct((bn, lt, D_KV), BF16),
            jax.ShapeDtypeStruct((bn, N_KV_HEADS, nt, D_KV, TOK_TILE), BF16),
            jax.ShapeDtypeStruct((bn, lt, D_ATTN), BF16),
            jax.ShapeDtypeStruct((bn, lt, D_LRU), BF16),
            coeff_shape, coeff_shape,
        ],
        scratch_shapes=[pltpu.VMEM((TOK_TILE, D_IN), F32), pltpu.VMEM((TOK_TILE, D_IN), F32),
                        pltpu.VMEM((SUBLANES, D_LRU), F32)],
        compiler_params=pltpu.CompilerParams(
            dimension_semantics=("arbitrary",), vmem_limit_bytes=VMEM_LIMIT),
        name="in_proj",
    )(*stream, ss, w_in, ang, anb, wcat, bs_full, qg, kg, cos_t, sin_t, obd, cw, cb, wg, bg, lam)


def _sigmoid(t):
    return 0.5 * jnp.tanh(0.5 * t) + 0.5


def _lru_kernel(a_ref, b_ref, gr_ref, o_ref, a_scr, b_scr, p_scr, h_scr):
    lt = gr_ref.shape[0]
    n_lat = lt - CTX_LEN
    n_chunk = lt // LRU_T
    n_ctx_chunk = CTX_LEN // LRU_T
    ctx_rows = LRU_SEG * CTX_PITCH
    lat_rows = LRU_SEG * LAT_PITCH
    assert CTX_LEN <= ctx_rows and n_lat <= lat_rows and (ctx_rows - CTX_LEN) % SUBLANES == 0
    regions = ((0, CTX_PITCH, CTX_LEN, 0), (ctx_rows, LAT_PITCH, n_lat, CTX_LEN))
    slabs = D_LRU // LANES

    for off, pitch, rows, tok0 in regions:
        n_pad = LRU_SEG * pitch - rows
        for d in range(N_DIR):
            for sl in range(slabs):
                a_scr[d, sl, off:off + rows, :] = a_ref[d, sl, tok0:tok0 + rows, :]
                b_scr[d, sl, off:off + rows, :] = b_ref[d, sl, tok0:tok0 + rows, :]
                a_scr[d, sl, off + rows:off + rows + n_pad, :] = jnp.ones((n_pad, LANES), F32)
                b_scr[d, sl, off + rows:off + rows + n_pad, :] = jnp.zeros((n_pad, LANES), F32)

    def sweep(off, pitch):
        def body(i, state):
            new = []
            for d in range(N_DIR):
                t = i if d == 0 else pitch - 1 - i
                idx = pl.ds(off + t, LRU_SEG, stride=pitch)
                for sl in range(slabs):
                    h, p = state[2 * (d * slabs + sl)], state[2 * (d * slabs + sl) + 1]
                    a = a_scr[d, sl, idx, :]
                    h = a * h + b_scr[d, sl, idx, :]
                    p = a * p
                    h_scr[d, sl, idx, :] = h
                    p_scr[d, sl, idx, :] = p
                    new += [h, p]
            return tuple(new)

        ident = (jnp.zeros((LRU_SEG, LANES), F32), jnp.ones((LRU_SEG, LANES), F32)) * (N_DIR * slabs)
        lax.fori_loop(0, pitch, body, ident, unroll=4)

    for off, pitch, _, _ in regions:
        sweep(off, pitch)

    def chain(d, off, pitch, init):
        carries = [None] * LRU_SEG
        c = init
        for j in (range(LRU_SEG) if d == 0 else reversed(range(LRU_SEG))):
            carries[j] = c
            last = off + j * pitch + (pitch - 1 if d == 0 else 0)
            c = [h_scr[d, sl, last:last + 1, :] + p_scr[d, sl, last:last + 1, :] * c[sl]
                 for sl in range(slabs)]
        return carries, c

    zero = [jnp.zeros((1, LANES), F32)] * slabs
    carries = []
    for d in range(N_DIR):
        c_ctx, fin = chain(d, regions[0][0], regions[0][1], zero)
        c_lat, _ = chain(d, regions[1][0], regions[1][1], fin)
        carries.append((c_ctx, c_lat))

    row = lax.broadcasted_iota(jnp.int32, (LRU_T, 1), 0)
    for ci in range(n_chunk):
        reg = 0 if ci < n_ctx_chunk else 1
        off, pitch, _, tok0 = regions[reg]
        r0 = ci * LRU_T - tok0
        j_lo, j_hi = r0 // pitch, (r0 + LRU_T - 1) // pitch
        for sl in range(slabs):
            rows = slice(off + r0, off + r0 + LRU_T)
            h = None
            for d in range(N_DIR):
                seg_c = carries[d][reg]
                c = seg_c[j_hi][sl]
                for j in range(j_hi - 1, j_lo - 1, -1):
                    c = jnp.where(row < (j + 1) * pitch - r0, seg_c[j][sl], c)
                hd = h_scr[d, sl, rows, :] + p_scr[d, sl, rows, :] * c
                h = hd if h is None else h + hd
            cols = slice(sl * LANES, (sl + 1) * LANES)
            gate = gr_ref[ci * LRU_T:(ci + 1) * LRU_T, cols].astype(F32)
            o_ref[ci * LRU_T:(ci + 1) * LRU_T, cols] = (h * gate).astype(BF16)


def _lru_call(a, b, gr):
    bn, lt, _ = gr.shape
    seq = pl.BlockSpec((None, lt, D_LRU), lambda i: (i, 0, 0))
    coeff = pl.BlockSpec((None,) + a.shape[1:], lambda i: (i, 0, 0, 0, 0))
    scan_rows = LRU_SEG * (CTX_PITCH + LAT_PITCH)
    return pl.pallas_call(
        _lru_kernel,
        grid=(bn,),
        in_specs=[coeff, coeff, seq],
        out_specs=seq,
        out_shape=jax.ShapeDtypeStruct((bn, lt, D_LRU), BF16),
        scratch_shapes=[pltpu.VMEM((N_DIR, D_LRU // LANES, scan_rows, LANES), F32)] * 4,
        compiler_params=pltpu.CompilerParams(
            dimension_semantics=("parallel",), vmem_limit_bytes=LRU_VMEM_LIMIT),
        name="rglru",
    )(a, b, gr)


def _attend(qt_ref, k_ref, va_ref, att_scr, rhs_scr, s_scr, p_scr, ot_scr, nk, bound, merge):
    pair = LANES // HEAD_DIM
    width = GQA_GROUP * Q_TILE
    n_chunk = nk // KEY_CHUNK
    units = [(qb, g) for qb in range(qt_ref.shape[1] // Q_TILE) for g in range(N_KV_HEADS)]
    n_unit = len(units)
    zeros = jnp.zeros((HEAD_DIM, width), BF16)

    def stage_rhs(u):
        qb, g = units[u]
        h0 = g * GQA_GROUP
        qt = jnp.concatenate(
            [qt_ref[(h0 + j) * HEAD_DIM:(h0 + j + 1) * HEAD_DIM, qb * Q_TILE:(qb + 1) * Q_TILE]
             for j in range(GQA_GROUP)], axis=1)
        rhs_scr[u % 2] = jnp.concatenate([qt, zeros] if g == 0 else [zeros, qt], axis=0)

    def score_chunk(u, c, mx):
        ks = pl.ds(pl.multiple_of(c * KEY_CHUNK, KEY_CHUNK), KEY_CHUNK)
        s = jnp.dot(k_ref[ks, :], rhs_scr[u % 2], preferred_element_type=F32)
        s_scr[u % 2, c] = s
        for i in range(KEY_CHUNK // SUBLANES):
            mx = jnp.maximum(mx, s[i * SUBLANES:(i + 1) * SUBLANES])
        return mx

    def exp_chunk(u, c, m):
        p_scr[u % 2, c] = jnp.exp2(s_scr[u % 2, c] - m).astype(BF16)

    def pv_chunk(u, c):
        g = units[u][1]
        ot_scr[...] += jnp.dot(va_ref[g, c], p_scr[u % 2, c], preferred_element_type=F32)

    def finish(u):
        qb, g = units[u]
        ot = ot_scr[...]
        if g == 0:
            o, l = ot[0:HEAD_DIM], ot[HEAD_DIM:HEAD_DIM + 1]
        else:
            o, l = ot[HEAD_DIM:2 * HEAD_DIM], ot[0:1]
        o = o * (1.0 / l)
        for rp in range(GQA_GROUP // pair):
            stacked = jnp.concatenate(
                [o[:, (rp * pair + j) * Q_TILE:(rp * pair + j + 1) * Q_TILE] for j in range(pair)], axis=0)
            col = (g * GQA_GROUP + rp * pair) // pair
            att_scr[qb * Q_TILE:(qb + 1) * Q_TILE, col * LANES:(col + 1) * LANES] = stacked.T
        if g == N_KV_HEADS - 1:
            merge(qb)

    def score_exp_chunk(u, c):
        ks = pl.ds(pl.multiple_of(c * KEY_CHUNK, KEY_CHUNK), KEY_CHUNK)
        s = jnp.dot(k_ref[ks, :], rhs_scr[u % 2], preferred_element_type=F32)
        p_scr[u % 2, c] = jnp.exp2(s - bound).astype(BF16)

    if bound is not None:
        for t in range(n_unit + 1):
            ua, uc = t, t - 1
            do_a, do_c = ua < n_unit, uc >= 0
            if do_a:
                stage_rhs(ua)
            if do_c:
                ot_scr[...] = jnp.zeros(ot_scr.shape, F32)

            def fast_body(c, carry, ua=ua, uc=uc, do_a=do_a, do_c=do_c):
                if do_a:
                    score_exp_chunk(ua, c)
                if do_c:
                    pv_chunk(uc, c)
                return carry

            lax.fori_loop(0, n_chunk, fast_body, 0, unroll=min(PIPE_UNROLL, n_chunk))
            if do_c:
                finish(uc)
        return

    col_max = {}
    for t in range(n_unit + 2):
        ua, ub, uc = t, t - 1, t - 2
        do_a, do_b, do_c = ua < n_unit, 0 <= ub < n_unit, 0 <= uc < n_unit
        if do_a:
            stage_rhs(ua)
        if do_c:
            ot_scr[...] = jnp.zeros(ot_scr.shape, F32)

        def body(c, mx, ua=ua, ub=ub, uc=uc, do_a=do_a, do_b=do_b, do_c=do_c):
            if do_a:
                mx = score_chunk(ua, c, mx)
            if do_b:
                exp_chunk(ub, c, col_max[ub])
            if do_c:
                pv_chunk(uc, c)
            return mx

        mx = lax.fori_loop(0, n_chunk, body, jnp.full((SUBLANES, width), -jnp.inf, F32),
                           unroll=min(PIPE_UNROLL, n_chunk))
        if do_a:
            col_max[ua] = jnp.max(mx, axis=0, keepdims=True)
        if do_c:
            finish(uc)


def _attn_kernel(flag_ref, bnd_ref, qt_ref, k_ref, va_ref, oa_ref, gb_ref, ol_ref, *refs,
                 q_off, with_ctx, split):
    if split:
        x_ref, ctx_ref, *refs = refs
    else:
        x_ref, *refs = refs
        ctx_ref = None
    mod_ref, wo_ref, lng_ref, lnb_ref, o_ref, att_scr, rhs_scr, s_scr, p_scr, ot_scr = refs
    nk_all = k_ref.shape[0]
    scratch = (att_scr, rhs_scr, s_scr, p_scr, ot_scr)
    bounded = flag_ref[0] == 1
    is_ctx = (pl.program_id(1) + q_off) < (CTX_LEN // Q_STEP)

    def merge(qb):
        rs = slice(qb * Q_TILE, (qb + 1) * Q_TILE)
        att = (att_scr[rs, :] * gb_ref[rs, :].astype(F32)).astype(BF16)
        cat = jnp.concatenate([oa_ref[rs, :], att, ol_ref[rs, :]], axis=1)
        y = jnp.dot(cat, wo_ref[...], preferred_element_type=F32)
        gate = mod_ref[:, 2 * D_MODEL:3 * D_MODEL]
        xt = _stream_tile(x_ref, ctx_ref, is_ctx, rs)
        o_ref[rs, :] = _layer_norm(ALPHA * xt + gate * y, lng_ref[...], lnb_ref[...])

    for ctx_step, nk in ((True, CTX_LEN), (False, nk_all)):
        if ctx_step and not with_ctx:
            continue
        here = (is_ctx if ctx_step else jnp.logical_not(is_ctx)) if with_ctx else True

        @pl.when(jnp.logical_and(here, bounded))
        def _():
            _attend(qt_ref, k_ref, va_ref, *scratch, nk, bnd_ref[...], merge)

        @pl.when(jnp.logical_and(here, jnp.logical_not(bounded)))
        def _():
            _attend(qt_ref, k_ref, va_ref, *scratch, nk, None, merge)


def _attn_call(flag, bnd, qt, k, va, oa, gb, ol, stream, ss, w_o, lng, lnb, last):
    split = isinstance(stream, tuple)
    assert not (split and last)
    bn, lt, _ = oa.shape
    n_ctx_step = CTX_LEN // Q_STEP
    q_off = n_ctx_step if last else 0
    nq = lt // Q_STEP - q_off
    tok = lambda width: pl.BlockSpec((None, Q_STEP, width), lambda b, i: (b, i + q_off, 0))
    full = lambda a: pl.BlockSpec(a.shape, lambda b, i: (0,) * a.ndim)
    if split:
        stream_specs = [
            pl.BlockSpec((None, Q_STEP, D_MODEL), lambda b, i: (b, jnp.maximum(i - 1, 0), 0)),
            pl.BlockSpec((None, CTX_LEN, D_MODEL), lambda b, i: (b, 0, 0)),
        ]
    else:
        stream = (stream,)
        stream_specs = [tok(D_MODEL)]
    return pl.pallas_call(
        functools.partial(_attn_kernel, q_off=q_off, with_ctx=not last, split=split),
        grid=(bn, nq),
        in_specs=[
            pl.BlockSpec(memory_space=pltpu.SMEM),
            full(bnd),
            pl.BlockSpec((None, D_ATTN, Q_STEP), lambda b, i: (b, 0, i + q_off)),
            pl.BlockSpec((None, lt, D_KV), lambda b, i: (b, 0, 0)),
            pl.BlockSpec((None, N_KV_HEADS, lt // KEY_CHUNK, D_KV, KEY_CHUNK), lambda b, i: (b, 0, 0, 0, 0)),
            tok(D_CHUNK), tok(D_ATTN), tok(D_LRU),
        ] + stream_specs + [
            pl.BlockSpec((None, None, 1, 3 * D_MODEL),
                         lambda b, i: (b, jnp.minimum((i + q_off) // n_ctx_step, 1), 0, 0)),
            full(w_o), full(lng), full(lnb),
        ],
        out_specs=pl.BlockSpec((None, Q_STEP, D_MODEL), lambda b, i: (b, i, 0)),
        out_shape=jax.ShapeDtypeStruct((bn, nq * Q_STEP, D_MODEL), F32),
        scratch_shapes=[
            pltpu.VMEM((Q_STEP, D_ATTN), F32),
            pltpu.VMEM((2, D_KV, GQA_GROUP * Q_TILE), BF16),
            pltpu.VMEM((2, lt // KEY_CHUNK, KEY_CHUNK, GQA_GROUP * Q_TILE), F32),
            pltpu.VMEM((2, lt // KEY_CHUNK, KEY_CHUNK, GQA_GROUP * Q_TILE), BF16),
            pltpu.VMEM((D_KV, GQA_GROUP * Q_TILE), F32),
        ],
        compiler_params=pltpu.CompilerParams(
            dimension_semantics=("parallel", "arbitrary"), vmem_limit_bytes=VMEM_LIMIT),
        name="attn_merge",
    )(flag, bnd, qt, k, va, oa, gb, ol, *stream, ss, w_o, lng, lnb)


def _rope_tables(n_lat):
    nf = HEAD_DIM // 4
    t = jnp.arange(n_lat, dtype=jnp.int32)
    pos = jnp.stack([t // GRID_W, t % GRID_W], axis=1).astype(F32)
    inv = ROPE_THETA ** (-jnp.arange(nf, dtype=F32) / nf)
    d = np.arange(HEAD_DIM)
    ang = pos[:, d // (HEAD_DIM // 2)] * inv[d % nf]
    sign = jnp.asarray(np.where((d % (HEAD_DIM // 2)) < nf, -1.0, 1.0), F32)
    cos = jnp.concatenate([jnp.ones((CTX_LEN, HEAD_DIM), F32), jnp.cos(ang)], axis=0)
    sin = jnp.concatenate([jnp.zeros((CTX_LEN, HEAD_DIM), F32), jnp.sin(ang) * sign], axis=0)
    reps = LANES // HEAD_DIM
    return jnp.tile(cos, (1, reps)), jnp.tile(sin, (1, reps))


def _block_diag(w):
    nb, n, _ = w.shape
    eye = jnp.eye(nb, dtype=w.dtype)
    return (eye[:, None, :, None] * w[:, :, None, :]).reshape(nb * n, nb * n)


def kernel(x, c, ctx, c_ctx, w_ada, b_ada, w_in, a_norm_g, a_norm_b, a_ws, a_bs, q_norm_g, k_norm_g,
           conv_w, conv_b, lru_wr, lru_br, lru_wi, lru_bi, lru_lam, w_o, ln_g, ln_b):
    bn, n_lat, _ = x.shape
    assert ctx.shape[1] == CTX_LEN and n_lat % TOK_TILE == 0 and bn + 1 <= MOD_ROWS
    xa = (x, ctx)

    cc = jnp.zeros((MOD_ROWS, D_MODEL), F32).at[:bn].set(c).at[bn].set(c_ctx)
    mod = _ada_call(cc, w_ada, b_ada)
    ctx_mod = jnp.broadcast_to(mod[:, bn:bn + 1], (DEPTH, bn, 3 * D_MODEL))
    ss = jnp.stack([ctx_mod, mod[:, :bn]], axis=2).reshape(DEPTH, bn, 2, 1, 3 * D_MODEL)

    cos_t, sin_t = _rope_tables(n_lat)
    head = np.arange(LANES) // HEAD_DIM
    obd = jnp.asarray(np.tile(head[:, None] == head[None, :], (2, 1)), BF16)

    w_in_b = w_in.astype(BF16)
    w_o_b = w_o.astype(BF16)
    for l in range(DEPTH):
        wcat = jnp.concatenate([a_ws[l, g] for g in range(A_GROUPS)], axis=1).astype(BF16)
        bs_full = jnp.repeat(a_bs[l].T, A_GDIM, axis=1)
        qg = (jnp.tile(q_norm_g[l], N_HEADS) * (HEAD_DIM ** -0.5 * LOG2E)).reshape(1, D_ATTN)
        kg = jnp.tile(k_norm_g[l], N_KV_HEADS).reshape(1, D_KV)
        wg = jnp.concatenate(
            [m for d in range(N_DIR) for m in (_block_diag(lru_wr[l, d]), _block_diag(lru_wi[l, d]))],
            axis=1).astype(BF16)
        bg = jnp.concatenate(
            [m for d in range(N_DIR) for m in (lru_br[l, d], lru_bi[l, d])]).reshape(1, 4 * D_LRU)
        oa, qt, k, va, gb, gr, lru_a, lru_b = _in_call(
            xa, ss[l], w_in_b[l], a_norm_g[l].reshape(1, D_CHUNK), a_norm_b[l].reshape(1, D_CHUNK),
            wcat, bs_full, qg, kg, cos_t, sin_t, obd,
            conv_w[l], conv_b[l].reshape(1, D_LRU), wg, bg, lru_lam[l])
        ol = _lru_call(lru_a, lru_b, gr)

        s_bound = (SCORE_BOUND_SLACK * HEAD_DIM ** 0.5 * LOG2E
                   * jnp.max(jnp.abs(q_norm_g[l])) * jnp.max(jnp.abs(k_norm_g[l])))
        flag = (s_bound <= MAX_SAFE_SCORE_BOUND).astype(jnp.int32).reshape(1)
        bnd = jnp.full((1, GQA_GROUP * Q_TILE), s_bound, F32)
        xa = _attn_call(flag, bnd, qt, k, va, oa, gb, ol, xa, ss[l], w_o_b[l], ln_g[l].reshape(1, D_MODEL),
                        ln_b[l].reshape(1, D_MODEL), last=(l == DEPTH - 1))
    return xa
```

```python
import functools

import jax
import jax.numpy as jnp
import numpy as np
from jax import lax
from jax.experimental import pallas as pl
from jax.experimental.pallas import tpu as pltpu

F32 = jnp.float32
BF16 = jnp.bfloat16

D_MODEL = 1024
DEPTH = 4
GRID_W = 64
CTX_LEN = 256
D_CHUNK = D_MODEL // 4
D_ATTN = D_MODEL // 2
D_LRU = D_MODEL // 4
CHUNK = 128
A_GROUPS = 4
A_GDIM = D_CHUNK // A_GROUPS
HEAD_DIM = 64
N_HEADS = D_ATTN // HEAD_DIM
N_KV_HEADS = N_HEADS // 4
GQA_GROUP = N_HEADS // N_KV_HEADS
D_KV = N_KV_HEADS * HEAD_DIM
ROPE_THETA = 10000.0
LRU_BLOCKS = 4
LRU_BDIM = D_LRU // LRU_BLOCKS
CONV_W = 4
LRU_C = 8.0
N_DIR = 2
D_IN = 3 * D_CHUNK + 2 * D_ATTN + 2 * D_KV + 2 * D_LRU
ALPHA = (2.0 * DEPTH) ** 0.25
LN_EPS = 1e-6
RMS_EPS = 1e-6
LOG2E = 1.4426950408889634
SCORE_BOUND_SLACK = 1.02
MAX_SAFE_SCORE_BOUND = 40.0

OFF_AU, OFF_AV, OFF_AG = 0, D_CHUNK, 2 * D_CHUNK
OFF_Q = 3 * D_CHUNK
OFF_K = OFF_Q + D_ATTN
OFF_V = OFF_K + D_KV
OFF_BG = OFF_V + D_KV
OFF_RX = OFF_BG + D_ATTN
OFF_RG = OFF_RX + D_LRU

LANES = 128
SUBLANES = 8
TOK_TILE = 256
PROJ_COLS = 512
MIX_PIECES = 11
Q_TILE = 128
Q_STEP = 256
PIPE_UNROLL = 9
KEY_CHUNK = 256
LRU_T = 128
LRU_SEG = 8
CTX_PITCH = 36
LAT_PITCH = 260
MOD_ROWS = 24
VMEM_LIMIT = 48 * 1024 * 1024
LRU_VMEM_LIMIT = 56 * 1024 * 1024


def _layer_norm(t, g, b):
    mu = jnp.mean(t, axis=-1, keepdims=True)
    d = t - mu
    var = jnp.mean(d * d, axis=-1, keepdims=True)
    return d * lax.rsqrt(var + LN_EPS) * g + b


def _ada_kernel(c_ref, w_ref, b_ref, o_ref):
    h = jax.nn.silu(c_ref[...]).astype(BF16)
    o_ref[...] = jnp.dot(h, w_ref[...].astype(BF16), preferred_element_type=F32) + b_ref[...]


def _ada_call(cc, w_ada, b_ada):
    n_col = 3 * D_MODEL // D_MODEL
    return pl.pallas_call(
        _ada_kernel,
        grid=(DEPTH, n_col),
        in_specs=[
            pl.BlockSpec((MOD_ROWS, D_MODEL), lambda l, j: (0, 0)),
            pl.BlockSpec((None, D_MODEL, D_MODEL), lambda l, j: (l, 0, j)),
            pl.BlockSpec((None, 1, D_MODEL), lambda l, j: (l, 0, j)),
        ],
        out_specs=pl.BlockSpec((None, MOD_ROWS, D_MODEL), lambda l, j: (l, 0, j)),
        out_shape=jax.ShapeDtypeStruct((DEPTH, MOD_ROWS, 3 * D_MODEL), F32),
        compiler_params=pltpu.CompilerParams(
            dimension_semantics=("arbitrary", "arbitrary"), vmem_limit_bytes=VMEM_LIMIT),
        name="adaln",
    )(cc, w_ada, b_ada.reshape(DEPTH, 1, 3 * D_MODEL))


def _head_sumsq(t, obd):
    sq = t * t
    hi = sq.astype(BF16)
    lo = (sq - hi.astype(F32)).astype(BF16)
    cols = []
    for c in range(t.shape[1] // LANES):
        sl = slice(c * LANES, (c + 1) * LANES)
        cols.append(jnp.dot(jnp.concatenate([hi[:, sl], lo[:, sl]], axis=1), obd,
                            preferred_element_type=F32))
    return cols[0] if len(cols) == 1 else jnp.concatenate(cols, axis=1)


def _rope(t, cos, sin, first_half):
    cols = []
    for c in range(t.shape[1] // LANES):
        xc = t[:, c * LANES:(c + 1) * LANES]
        up = pltpu.roll(xc, LANES - HEAD_DIM // 4, 1)
        dn = pltpu.roll(xc, HEAD_DIM // 4, 1)
        cols.append(xc * cos + jnp.where(first_half, up, dn) * sin)
    return cols[0] if len(cols) == 1 else jnp.concatenate(cols, axis=1)


def _silu(t):
    return t * _sigmoid(t)


def _lru_coefficients(w, cw_ref, cb_ref, wg_ref, bg_ref, lam_ref, a_ref, b_ref):
    pad = SUBLANES
    win = w.shape[0]
    n = win - 2 * pad
    cw = cw_ref[...]
    xr = cb_ref[...] + cw[2:3] * w[pad:pad + n]
    xr = xr + cw[0:1] * pltpu.roll(w, 2, 0)[pad:pad + n]
    xr = xr + cw[1:2] * pltpu.roll(w, 1, 0)[pad:pad + n]
    xr = xr + cw[3:4] * pltpu.roll(w, win - 1, 0)[pad:pad + n]
    g = jnp.dot(xr.astype(BF16), wg_ref[...], preferred_element_type=F32) + bg_ref[...]
    half_x = 0.5 * xr
    for d in range(N_DIR):
        yield
        nl = -lam_ref[d:d + 1, :]
        softplus = jnp.maximum(nl, 0.0) + jnp.log1p(jnp.exp(-jnp.abs(nl)))
        half_c_sp = (0.5 * LRU_C) * softplus
        t_r = jnp.tanh(0.5 * g[:, 2 * d * D_LRU:(2 * d + 1) * D_LRU])
        t_i = jnp.tanh(0.5 * g[:, (2 * d + 1) * D_LRU:(2 * d + 2) * D_LRU])
        neg_log_a = half_c_sp * t_r + half_c_sp
        a = jnp.exp2(neg_log_a * (-LOG2E))
        y = jnp.tanh(neg_log_a) * (1.0 + a * a)
        root = jnp.where(y > 0.0, y * lax.rsqrt(y), 0.0)
        b = root * (half_x * t_i + half_x)
        for sl in range(D_LRU // LANES):
            a_ref[d, sl] = a[:, sl * LANES:(sl + 1) * LANES]
            b_ref[d, sl] = b[:, sl * LANES:(sl + 1) * LANES]


def _in_epilogue(z_ref, z_next_ref, seg_start, seg_end, carry_scr,
                 ang_ref, anb_ref, wcat_ref, bs_ref, qg_ref, kg_ref, cos_ref, sin_ref, obd_ref,
                 cw_ref, cb_ref, wg_ref, bg_ref, lam_ref,
                 oa_ref, qt_ref, k_ref, va_ref, gb_ref, gr_ref, a_ref, b_ref):
    rx = z_ref[:, OFF_RX:OFF_RX + D_LRU]
    head = jnp.where(seg_start, 0.0, carry_scr[...])
    tail = jnp.where(seg_end, 0.0, z_next_ref[0:SUBLANES, OFF_RX:OFF_RX + D_LRU])
    carry_scr[...] = rx[TOK_TILE - SUBLANES:]
    yield from _lru_coefficients(jnp.concatenate([head, rx, tail], axis=0),
                                 cw_ref, cb_ref, wg_ref, bg_ref, lam_ref, a_ref, b_ref)
    yield

    u = jax.nn.gelu(z_ref[:, OFF_AU:OFF_AU + D_CHUNK])
    yield
    vn = _layer_norm(jax.nn.gelu(z_ref[:, OFF_AV:OFF_AV + D_CHUNK]), ang_ref[...], anb_ref[...])
    yield
    ga = _silu(z_ref[:, OFF_AG:OFF_AG + D_CHUNK])
    grp = lax.broadcasted_iota(jnp.int32, (1, D_CHUNK), 1) // A_GDIM
    for ch in range(TOK_TILE // CHUNK):
        rs = slice(ch * CHUNK, (ch + 1) * CHUNK)
        vc = vn[rs]
        vstack = jnp.concatenate(
            [jnp.where(grp == g, vc, 0.0) for g in range(A_GROUPS)], axis=0).astype(BF16)
        s = jnp.dot(wcat_ref[...], vstack, preferred_element_type=F32) + bs_ref[...]
        oa_ref[rs, :] = (u[rs] * s * ga[rs]).astype(BF16)
    yield

    obd = obd_ref[...]
    cos = cos_ref[...]
    sin = sin_ref[...]
    lane = lax.broadcasted_iota(jnp.int32, (1, LANES), 1)
    first_half = (lane % (HEAD_DIM // 2)) < (HEAD_DIM // 4)
    q = z_ref[:, OFF_Q:OFF_Q + D_ATTN]
    q = q * lax.rsqrt(_head_sumsq(q, obd) * (1.0 / HEAD_DIM) + RMS_EPS) * qg_ref[...]
    yield
    qt_ref[...] = _rope(q, cos, sin, first_half).T.astype(BF16)
    yield
    k = z_ref[:, OFF_K:OFF_K + D_KV]
    k = k * lax.rsqrt(_head_sumsq(k, obd) * (1.0 / HEAD_DIM) + RMS_EPS) * kg_ref[...]
    k_ref[...] = _rope(k, cos, sin, first_half).astype(BF16)
    vt = z_ref[:, OFF_V:OFF_V + D_KV].T.astype(BF16)
    ones = jnp.ones((HEAD_DIM, TOK_TILE), BF16)
    va_ref[0, 0] = jnp.concatenate([vt[0:HEAD_DIM], ones], axis=0)
    va_ref[1, 0] = jnp.concatenate([ones, vt[HEAD_DIM:2 * HEAD_DIM]], axis=0)
    yield
    gb_ref[...] = _silu(z_ref[:, OFF_BG:OFF_BG + D_ATTN]).astype(BF16)
    yield

    gr_ref[...] = _silu(z_ref[:, OFF_RG:OFF_RG + D_LRU]).astype(BF16)


def _stream_tile(x_ref, ctx_ref, is_ctx, rows=slice(None)):
    if ctx_ref is None:
        return x_ref[rows, :]
    return jnp.where(is_ctx, ctx_ref[rows, :], x_ref[rows, :])


def _in_kernel(*refs, nt, split):
    if split:
        x_ref, ctx_ref, mod_ref, w_ref, *rest = refs
    else:
        x_ref, mod_ref, w_ref, *rest = refs
        ctx_ref = None
    *epi_refs, z0_scr, z1_scr, carry_scr = rest
    i = pl.program_id(0)
    n_tile = pl.num_programs(0) - 1
    t_prev = jnp.maximum(i - 1, 0) % nt
    seg_start = jnp.logical_or(t_prev == 0, t_prev == CTX_LEN // TOK_TILE)
    seg_end = jnp.logical_or(t_prev == CTX_LEN // TOK_TILE - 1, t_prev == nt - 1)

    @pl.when(i == 0)
    def _():
        z1_scr[...] = jnp.zeros(z1_scr.shape, F32)
        carry_scr[...] = jnp.zeros(carry_scr.shape, F32)

    def project(z_new):
        mod = mod_ref[...]
        shift = mod[:, 0:D_MODEL]
        scale = mod[:, D_MODEL:2 * D_MODEL]
        xt = _stream_tile(x_ref, ctx_ref, jnp.minimum(i, n_tile - 1) % nt == 0)
        xm = (xt * (1.0 + scale) + shift).astype(BF16)
        n_col = D_IN // PROJ_COLS
        for c in [OFF_RX // PROJ_COLS] + [c for c in range(n_col) if c != OFF_RX // PROJ_COLS]:
            cs = slice(c * PROJ_COLS, (c + 1) * PROJ_COLS)
            z_new[:, cs] = jnp.dot(xm, w_ref[:, cs], preferred_element_type=F32)
            yield

    def step(z_new, z_prev):
        proj = project(z_new)
        mix = _in_epilogue(z_prev, z_new, seg_start, seg_end, carry_scr, *epi_refs)
        n_col = D_IN // PROJ_COLS
        done = 0
        next(proj)
        for c in range(1, n_col):
            while done * (n_col - 1) < c * MIX_PIECES:
                next(mix, None)
                done += 1
            next(proj)
        assert next(proj, None) is None and next(mix, None) is None

    @pl.when(i % 2 == 0)
    def _():
        step(z0_scr, z1_scr)

    @pl.when(i % 2 == 1)
    def _():
        step(z1_scr, z0_scr)


def _in_call(stream, ss, w_in, ang, anb, wcat, bs_full, qg, kg, cos_t, sin_t, obd, cw, cb, wg, bg, lam):
    split = isinstance(stream, tuple)
    bn, lt = stream[0].shape[:2] if split else stream.shape[:2]
    lt += CTX_LEN if split else 0
    nt = lt // TOK_TILE
    n_tile = bn * nt
    cur = lambda i: jnp.minimum(i, n_tile - 1)
    prev = lambda i: jnp.maximum(i - 1, 0)
    tok = lambda width: pl.BlockSpec((None, TOK_TILE, width), lambda i: (prev(i) // nt, prev(i) % nt, 0))
    full = lambda a: pl.BlockSpec(a.shape, lambda i: (0,) * a.ndim)
    slabs = D_LRU // LANES
    coeff_spec = pl.BlockSpec((None, N_DIR, slabs, TOK_TILE, LANES), lambda i: (prev(i) // nt, 0, 0, prev(i) % nt, 0))
    coeff_shape = jax.ShapeDtypeStruct((bn, N_DIR, slabs, lt, LANES), F32)
    if split:
        stream_specs = [
            pl.BlockSpec((None, TOK_TILE, D_MODEL), lambda i: (cur(i) // nt, jnp.maximum(cur(i) % nt - 1, 0), 0)),
            pl.BlockSpec((None, CTX_LEN, D_MODEL), lambda i: (cur(i) // nt, 0, 0)),
        ]
    else:
        stream = (stream,)
        stream_specs = [pl.BlockSpec((None, TOK_TILE, D_MODEL), lambda i: (cur(i) // nt, cur(i) % nt, 0))]
    return pl.pallas_call(
        functools.partial(_in_kernel, nt=nt, split=split),
        grid=(n_tile + 1,),
        in_specs=stream_specs + [
            pl.BlockSpec((None, None, 1, 3 * D_MODEL),
                         lambda i: (cur(i) // nt, jnp.minimum(cur(i) % nt, 1), 0, 0)),
            full(w_in), full(ang), full(anb), full(wcat), full(bs_full), full(qg), full(kg),
            pl.BlockSpec((TOK_TILE, LANES), lambda i: (prev(i) % nt, 0)),
            pl.BlockSpec((TOK_TILE, LANES), lambda i: (prev(i) % nt, 0)),
            full(obd), full(cw), full(cb), full(wg), full(bg), full(lam),
        ],
        out_specs=[
            tok(D_CHUNK),
            pl.BlockSpec((None, D_ATTN, TOK_TILE), lambda i: (prev(i) // nt, 0, prev(i) % nt)),
            tok(D_KV),
            pl.BlockSpec((None, N_KV_HEADS, 1, D_KV, TOK_TILE),
                         lambda i: (prev(i) // nt, 0, prev(i) % nt, 0, 0)),
            tok(D_ATTN), tok(D_LRU), coeff_spec, coeff_spec,
        ],
        out_shape=[
            jax.ShapeDtypeStruct((bn, lt, D_CHUNK), BF16),
            jax.ShapeDtypeStruct((bn, D_ATTN, lt), BF16),
            jax.ShapeDtypeStruct((bn, lt, D_KV), BF16),
            jax.ShapeDtypeStruct((bn, N_KV_HEADS, nt, D_KV, TOK_TILE), BF16),
            jax.ShapeDtypeStruct((bn, lt, D_ATTN), BF16),
            jax.ShapeDtypeStruct((bn, lt, D_LRU), BF16),
            coeff_shape, coeff_shape,
        ],
        scratch_shapes=[pltpu.VMEM((TOK_TILE, D_IN), F32), pltpu.VMEM((TOK_TILE, D_IN), F32),
                        pltpu.VMEM((SUBLANES, D_LRU), F32)],
        compiler_params=pltpu.CompilerParams(
            dimension_semantics=("arbitrary",), vmem_limit_bytes=VMEM_LIMIT),
        name="in_proj",
    )(*stream, ss, w_in, ang, anb, wcat, bs_full, qg, kg, cos_t, sin_t, obd, cw, cb, wg, bg, lam)


def _sigmoid(t):
    return 0.5 * jnp.tanh(0.5 * t) + 0.5


def _lru_kernel(a_ref, b_ref, gr_ref, o_ref, a_scr, b_scr, p_scr, h_scr):
    lt = gr_ref.shape[0]
    n_lat = lt - CTX_LEN
    n_chunk = lt // LRU_T
    n_ctx_chunk = CTX_LEN // LRU_T
    ctx_rows = LRU_SEG * CTX_PITCH
    lat_rows = LRU_SEG * LAT_PITCH
    assert CTX_LEN <= ctx_rows and n_lat <= lat_rows and (ctx_rows - CTX_LEN) % SUBLANES == 0
    regions = ((0, CTX_PITCH, CTX_LEN, 0), (ctx_rows, LAT_PITCH, n_lat, CTX_LEN))
    slabs = D_LRU // LANES

    for off, pitch, rows, tok0 in regions:
        n_pad = LRU_SEG * pitch - rows
        for d in range(N_DIR):
            for sl in range(slabs):
                a_scr[d, sl, off:off + rows, :] = a_ref[d, sl, tok0:tok0 + rows, :]
                b_scr[d, sl, off:off + rows, :] = b_ref[d, sl, tok0:tok0 + rows, :]
                a_scr[d, sl, off + rows:off + rows + n_pad, :] = jnp.ones((n_pad, LANES), F32)
                b_scr[d, sl, off + rows:off + rows + n_pad, :] = jnp.zeros((n_pad, LANES), F32)

    def sweep(off, pitch):
        def body(i, state):
            new = []
            for d in range(N_DIR):
                t = i if d == 0 else pitch - 1 - i
                idx = pl.ds(off + t, LRU_SEG, stride=pitch)
                for sl in range(slabs):
                    h, p = state[2 * (d * slabs + sl)], state[2 * (d * slabs + sl) + 1]
                    a = a_scr[d, sl, idx, :]
                    h = a * h + b_scr[d, sl, idx, :]
                    p = a * p
                    h_scr[d, sl, idx, :] = h
                    p_scr[d, sl, idx, :] = p
                    new += [h, p]
            return tuple(new)

        ident = (jnp.zeros((LRU_SEG, LANES), F32), jnp.ones((LRU_SEG, LANES), F32)) * (N_DIR * slabs)
        lax.fori_loop(0, pitch, body, ident, unroll=4)

    for off, pitch, _, _ in regions:
        sweep(off, pitch)

    def chain(d, off, pitch, init):
        carries = [None] * LRU_SEG
        c = init
        for j in (range(LRU_SEG) if d == 0 else reversed(range(LRU_SEG))):
            carries[j] = c
            last = off + j * pitch + (pitch - 1 if d == 0 else 0)
            c = [h_scr[d, sl, last:last + 1, :] + p_scr[d, sl, last:last + 1, :] * c[sl]
                 for sl in range(slabs)]
        return carries, c

    zero = [jnp.zeros((1, LANES), F32)] * slabs
    carries = []
    for d in range(N_DIR):
        c_ctx, fin = chain(d, regions[0][0], regions[0][1], zero)
        c_lat, _ = chain(d, regions[1][0], regions[1][1], fin)
        carries.append((c_ctx, c_lat))

    row = lax.broadcasted_iota(jnp.int32, (LRU_T, 1), 0)
    for ci in range(n_chunk):
        reg = 0 if ci < n_ctx_chunk else 1
        off, pitch, _, tok0 = regions[reg]
        r0 = ci * LRU_T - tok0
        j_lo, j_hi = r0 // pitch, (r0 + LRU_T - 1) // pitch
        for sl in range(slabs):
            rows = slice(off + r0, off + r0 + LRU_T)
            h = None
            for d in range(N_DIR):
                seg_c = carries[d][reg]
                c = seg_c[j_hi][sl]
                for j in range(j_hi - 1, j_lo - 1, -1):
                    c = jnp.where(row < (j + 1) * pitch - r0, seg_c[j][sl], c)
                hd = h_scr[d, sl, rows, :] + p_scr[d, sl, rows, :] * c
                h = hd if h is None else h + hd
            cols = slice(sl * LANES, (sl + 1) * LANES)
            gate = gr_ref[ci * LRU_T:(ci + 1) * LRU_T, cols].astype(F32)
            o_ref[ci * LRU_T:(ci + 1) * LRU_T, cols] = (h * gate).astype(BF16)


def _lru_call(a, b, gr):
    bn, lt, _ = gr.shape
    seq = pl.BlockSpec((None, lt, D_LRU), lambda i: (i, 0, 0))
    coeff = pl.BlockSpec((None,) + a.shape[1:], lambda i: (i, 0, 0, 0, 0))
    scan_rows = LRU_SEG * (CTX_PITCH + LAT_PITCH)
    return pl.pallas_call(
        _lru_kernel,
        grid=(bn,),
        in_specs=[coeff, coeff, seq],
        out_specs=seq,
        out_shape=jax.ShapeDtypeStruct((bn, lt, D_LRU), BF16),
        scratch_shapes=[pltpu.VMEM((N_DIR, D_LRU // LANES, scan_rows, LANES), F32)] * 4,
        compiler_params=pltpu.CompilerParams(
            dimension_semantics=("parallel",), vmem_limit_bytes=LRU_VMEM_LIMIT),
        name="rglru",
    )(a, b, gr)


def _attend(qt_ref, k_ref, va_ref, att_scr, rhs_scr, s_scr, p_scr, ot_scr, nk, bound):
    pair = LANES // HEAD_DIM
    width = GQA_GROUP * Q_TILE
    n_chunk = nk // KEY_CHUNK
    units = [(qb, g) for qb in range(qt_ref.shape[1] // Q_TILE) for g in range(N_KV_HEADS)]
    n_unit = len(units)
    zeros = jnp.zeros((HEAD_DIM, width), BF16)

    def stage_rhs(u):
        qb, g = units[u]
        h0 = g * GQA_GROUP
        qt = jnp.concatenate(
            [qt_ref[(h0 + j) * HEAD_DIM:(h0 + j + 1) * HEAD_DIM, qb * Q_TILE:(qb + 1) * Q_TILE]
             for j in range(GQA_GROUP)], axis=1)
        rhs_scr[u % 2] = jnp.concatenate([qt, zeros] if g == 0 else [zeros, qt], axis=0)

    def score_chunk(u, c, mx):
        ks = pl.ds(pl.multiple_of(c * KEY_CHUNK, KEY_CHUNK), KEY_CHUNK)
        s = jnp.dot(k_ref[ks, :], rhs_scr[u % 2], preferred_element_type=F32)
        s_scr[u % 2, c] = s
        for i in range(KEY_CHUNK // SUBLANES):
            mx = jnp.maximum(mx, s[i * SUBLANES:(i + 1) * SUBLANES])
        return mx

    def exp_chunk(u, c, m):
        p_scr[u % 2, c] = jnp.exp2(s_scr[u % 2, c] - m).astype(BF16)

    def pv_chunk(u, c):
        g = units[u][1]
        ot_scr[...] += jnp.dot(va_ref[g, c], p_scr[u % 2, c], preferred_element_type=F32)

    def finish(u):
        qb, g = units[u]
        ot = ot_scr[...]
        if g == 0:
            o, l = ot[0:HEAD_DIM], ot[HEAD_DIM:HEAD_DIM + 1]
        else:
            o, l = ot[HEAD_DIM:2 * HEAD_DIM], ot[0:1]
        o = o * (1.0 / l)
        for rp in range(GQA_GROUP // pair):
            stacked = jnp.concatenate(
                [o[:, (rp * pair + j) * Q_TILE:(rp * pair + j + 1) * Q_TILE] for j in range(pair)], axis=0)
            col = (g * GQA_GROUP + rp * pair) // pair
            att_scr[qb * Q_TILE:(qb + 1) * Q_TILE, col * LANES:(col + 1) * LANES] = stacked.T

    def score_exp_chunk(u, c):
        ks = pl.ds(pl.multiple_of(c * KEY_CHUNK, KEY_CHUNK), KEY_CHUNK)
        s = jnp.dot(k_ref[ks, :], rhs_scr[u % 2], preferred_element_type=F32)
        p_scr[u % 2, c] = jnp.exp2(s - bound).astype(BF16)

    if bound is not None:
        for t in range(n_unit + 1):
            ua, uc = t, t - 1
            do_a, do_c = ua < n_unit, uc >= 0
            if do_a:
                stage_rhs(ua)
            if do_c:
                ot_scr[...] = jnp.zeros(ot_scr.shape, F32)

            def fast_body(c, carry, ua=ua, uc=uc, do_a=do_a, do_c=do_c):
                if do_a:
                    score_exp_chunk(ua, c)
                if do_c:
                    pv_chunk(uc, c)
                return carry

            lax.fori_loop(0, n_chunk, fast_body, 0, unroll=min(PIPE_UNROLL, n_chunk))
            if do_c:
                finish(uc)
        return

    col_max = {}
    for t in range(n_unit + 2):
        ua, ub, uc = t, t - 1, t - 2
        do_a, do_b, do_c = ua < n_unit, 0 <= ub < n_unit, 0 <= uc < n_unit
        if do_a:
            stage_rhs(ua)
        if do_c:
            ot_scr[...] = jnp.zeros(ot_scr.shape, F32)

        def body(c, mx, ua=ua, ub=ub, uc=uc, do_a=do_a, do_b=do_b, do_c=do_c):
            if do_a:
                mx = score_chunk(ua, c, mx)
            if do_b:
                exp_chunk(ub, c, col_max[ub])
            if do_c:
                pv_chunk(uc, c)
            return mx

        mx = lax.fori_loop(0, n_chunk, body, jnp.full((SUBLANES, width), -jnp.inf, F32),
                           unroll=min(PIPE_UNROLL, n_chunk))
        if do_a:
            col_max[ua] = jnp.max(mx, axis=0, keepdims=True)
        if do_c:
            finish(uc)


def _attn_kernel(flag_ref, bnd_ref, qt_ref, k_ref, va_ref, oa_ref, gb_ref, ol_ref, *refs,
                 q_off, with_ctx, split):
    if split:
        x_ref, ctx_ref, *refs = refs
    else:
        x_ref, *refs = refs
        ctx_ref = None
    mod_ref, wo_ref, lng_ref, lnb_ref, o_ref, att_scr, rhs_scr, s_scr, p_scr, ot_scr = refs
    nk_all = k_ref.shape[0]
    scratch = (att_scr, rhs_scr, s_scr, p_scr, ot_scr)
    bounded = flag_ref[0] == 1
    is_ctx = (pl.program_id(1) + q_off) < (CTX_LEN // Q_STEP)

    for ctx_step, nk in ((True, CTX_LEN), (False, nk_all)):
        if ctx_step and not with_ctx:
            continue
        here = (is_ctx if ctx_step else jnp.logical_not(is_ctx)) if with_ctx else True

        @pl.when(jnp.logical_and(here, bounded))
        def _():
            _attend(qt_ref, k_ref, va_ref, *scratch, nk, bnd_ref[...])

        @pl.when(jnp.logical_and(here, jnp.logical_not(bounded)))
        def _():
            _attend(qt_ref, k_ref, va_ref, *scratch, nk, None)

    att = (att_scr[...] * gb_ref[...].astype(F32)).astype(BF16)
    cat = jnp.concatenate([oa_ref[...], att, ol_ref[...]], axis=1)
    y = jnp.dot(cat, wo_ref[...], preferred_element_type=F32)
    gate = mod_ref[:, 2 * D_MODEL:3 * D_MODEL]
    xt = _stream_tile(x_ref, ctx_ref, is_ctx)
    o_ref[...] = _layer_norm(ALPHA * xt + gate * y, lng_ref[...], lnb_ref[...])


def _attn_call(flag, bnd, qt, k, va, oa, gb, ol, stream, ss, w_o, lng, lnb, last):
    split = isinstance(stream, tuple)
    assert not (split and last)
    bn, lt, _ = oa.shape
    n_ctx_step = CTX_LEN // Q_STEP
    q_off = n_ctx_step if last else 0
    nq = lt // Q_STEP - q_off
    tok = lambda width: pl.BlockSpec((None, Q_STEP, width), lambda b, i: (b, i + q_off, 0))
    full = lambda a: pl.BlockSpec(a.shape, lambda b, i: (0,) * a.ndim)
    if split:
        stream_specs = [
            pl.BlockSpec((None, Q_STEP, D_MODEL), lambda b, i: (b, jnp.maximum(i - 1, 0), 0)),
            pl.BlockSpec((None, CTX_LEN, D_MODEL), lambda b, i: (b, 0, 0)),
        ]
    else:
        stream = (stream,)
        stream_specs = [tok(D_MODEL)]
    return pl.pallas_call(
        functools.partial(_attn_kernel, q_off=q_off, with_ctx=not last, split=split),
        grid=(bn, nq),
        in_specs=[
            pl.BlockSpec(memory_space=pltpu.SMEM),
            full(bnd),
            pl.BlockSpec((None, D_ATTN, Q_STEP), lambda b, i: (b, 0, i + q_off)),
            pl.BlockSpec((None, lt, D_KV), lambda b, i: (b, 0, 0)),
            pl.BlockSpec((None, N_KV_HEADS, lt // KEY_CHUNK, D_KV, KEY_CHUNK), lambda b, i: (b, 0, 0, 0, 0)),
            tok(D_CHUNK), tok(D_ATTN), tok(D_LRU),
        ] + stream_specs + [
            pl.BlockSpec((None, None, 1, 3 * D_MODEL),
                         lambda b, i: (b, jnp.minimum((i + q_off) // n_ctx_step, 1), 0, 0)),
            full(w_o), full(lng), full(lnb),
        ],
        out_specs=pl.BlockSpec((None, Q_STEP, D_MODEL), lambda b, i: (b, i, 0)),
        out_shape=jax.ShapeDtypeStruct((bn, nq * Q_STEP, D_MODEL), F32),
        scratch_shapes=[
            pltpu.VMEM((Q_STEP, D_ATTN), F32),
            pltpu.VMEM((2, D_KV, GQA_GROUP * Q_TILE), BF16),
            pltpu.VMEM((2, lt // KEY_CHUNK, KEY_CHUNK, GQA_GROUP * Q_TILE), F32),
            pltpu.VMEM((2, lt // KEY_CHUNK, KEY_CHUNK, GQA_GROUP * Q_TILE), BF16),
            pltpu.VMEM((D_KV, GQA_GROUP * Q_TILE), F32),
        ],
        compiler_params=pltpu.CompilerParams(
            dimension_semantics=("parallel", "arbitrary"), vmem_limit_bytes=VMEM_LIMIT),
        name="attn_merge",
    )(flag, bnd, qt, k, va, oa, gb, ol, *stream, ss, w_o, lng, lnb)


def _rope_tables(n_lat):
    nf = HEAD_DIM // 4
    t = jnp.arange(n_lat, dtype=jnp.int32)
    pos = jnp.stack([t // GRID_W, t % GRID_W], axis=1).astype(F32)
    inv = ROPE_THETA ** (-jnp.arange(nf, dtype=F32) / nf)
    d = np.arange(HEAD_DIM)
    ang = pos[:, d // (HEAD_DIM // 2)] * inv[d % nf]
    sign = jnp.asarray(np.where((d % (HEAD_DIM // 2)) < nf, -1.0, 1.0), F32)
    cos = jnp.concatenate([jnp.ones((CTX_LEN, HEAD_DIM), F32), jnp.cos(ang)], axis=0)
    sin = jnp.concatenate([jnp.zeros((CTX_LEN, HEAD_DIM), F32), jnp.sin(ang) * sign], axis=0)
    reps = LANES // HEAD_DIM
    return jnp.tile(cos, (1, reps)), jnp.tile(sin, (1, reps))


def _block_diag(w):
    nb, n, _ = w.shape
    eye = jnp.eye(nb, dtype=w.dtype)
    return (eye[:, None, :, None] * w[:, :, None, :]).reshape(nb * n, nb * n)


def kernel(x, c, ctx, c_ctx, w_ada, b_ada, w_in, a_norm_g, a_norm_b, a_ws, a_bs, q_norm_g, k_norm_g,
           conv_w, conv_b, lru_wr, lru_br, lru_wi, lru_bi, lru_lam, w_o, ln_g, ln_b):
    bn, n_lat, _ = x.shape
    assert ctx.shape[1] == CTX_LEN and n_lat % TOK_TILE == 0 and bn + 1 <= MOD_ROWS
    xa = (x, ctx)

    cc = jnp.zeros((MOD_ROWS, D_MODEL), F32).at[:bn].set(c).at[bn].set(c_ctx)
    mod = _ada_call(cc, w_ada, b_ada)
    ctx_mod = jnp.broadcast_to(mod[:, bn:bn + 1], (DEPTH, bn, 3 * D_MODEL))
    ss = jnp.stack([ctx_mod, mod[:, :bn]], axis=2).reshape(DEPTH, bn, 2, 1, 3 * D_MODEL)

    cos_t, sin_t = _rope_tables(n_lat)
    head = np.arange(LANES) // HEAD_DIM
    obd = jnp.asarray(np.tile(head[:, None] == head[None, :], (2, 1)), BF16)

    w_in_b = w_in.astype(BF16)
    w_o_b = w_o.astype(BF16)
    for l in range(DEPTH):
        wcat = jnp.concatenate([a_ws[l, g] for g in range(A_GROUPS)], axis=1).astype(BF16)
        bs_full = jnp.repeat(a_bs[l].T, A_GDIM, axis=1)
        qg = (jnp.tile(q_norm_g[l], N_HEADS) * (HEAD_DIM ** -0.5 * LOG2E)).reshape(1, D_ATTN)
        kg = jnp.tile(k_norm_g[l], N_KV_HEADS).reshape(1, D_KV)
        wg = jnp.concatenate(
            [m for d in range(N_DIR) for m in (_block_diag(lru_wr[l, d]), _block_diag(lru_wi[l, d]))],
            axis=1).astype(BF16)
        bg = jnp.concatenate(
            [m for d in range(N_DIR) for m in (lru_br[l, d], lru_bi[l, d])]).reshape(1, 4 * D_LRU)
        oa, qt, k, va, gb, gr, lru_a, lru_b = _in_call(
            xa, ss[l], w_in_b[l], a_norm_g[l].reshape(1, D_CHUNK), a_norm_b[l].reshape(1, D_CHUNK),
            wcat, bs_full, qg, kg, cos_t, sin_t, obd,
            conv_w[l], conv_b[l].reshape(1, D_LRU), wg, bg, lru_lam[l])
        ol = _lru_call(lru_a, lru_b, gr)

        s_bound = (SCORE_BOUND_SLACK * HEAD_DIM ** 0.5 * LOG2E
                   * jnp.max(jnp.abs(q_norm_g[l])) * jnp.max(jnp.abs(k_norm_g[l])))
        flag = (s_bound <= MAX_SAFE_SCORE_BOUND).astype(jnp.int32).reshape(1)
        bnd = jnp.full((1, GQA_GROUP * Q_TILE), s_bound, F32)
        xa = _attn_call(flag, bnd, qt, k, va, oa, gb, ol, xa, ss[l], w_o_b[l], ln_g[l].reshape(1, D_MODEL),
                        ln_b[l].reshape(1, D_MODEL), last=(l == DEPTH - 1))
    return xa
```

```python
import functools

import jax
import jax.numpy as jnp
import numpy as np
from jax import lax
from jax.experimental import pallas as pl
from jax.experimental.pallas import tpu as pltpu

F32 = jnp.float32
BF16 = jnp.bfloat16

D_MODEL = 1024
DEPTH = 4
GRID_W = 64
CTX_LEN = 256
D_CHUNK = D_MODEL // 4
D_ATTN = D_MODEL // 2
D_LRU = D_MODEL // 4
CHUNK = 128
A_GROUPS = 4
A_GDIM = D_CHUNK // A_GROUPS
HEAD_DIM = 64
N_HEADS = D_ATTN // HEAD_DIM
N_KV_HEADS = N_HEADS // 4
GQA_GROUP = N_HEADS // N_KV_HEADS
D_KV = N_KV_HEADS * HEAD_DIM
ROPE_THETA = 10000.0
LRU_BLOCKS = 4
LRU_BDIM = D_LRU // LRU_BLOCKS
CONV_W = 4
LRU_C = 8.0
N_DIR = 2
D_IN = 3 * D_CHUNK + 2 * D_ATTN + 2 * D_KV + 2 * D_LRU
ALPHA = (2.0 * DEPTH) ** 0.25
LN_EPS = 1e-6
RMS_EPS = 1e-6
LOG2E = 1.4426950408889634
SCORE_BOUND_SLACK = 1.02
MAX_SAFE_SCORE_BOUND = 40.0

OFF_AU, OFF_AV, OFF_AG = 0, D_CHUNK, 2 * D_CHUNK
OFF_Q = 3 * D_CHUNK
OFF_K = OFF_Q + D_ATTN
OFF_V = OFF_K + D_KV
OFF_BG = OFF_V + D_KV
OFF_RX = OFF_BG + D_ATTN
OFF_RG = OFF_RX + D_LRU

LANES = 128
SUBLANES = 8
TOK_TILE = 256
PROJ_COLS = 512
Q_TILE = 128
Q_STEP = 256
PIPE_UNROLL = 9
KEY_CHUNK = 256
LRU_T = 128
LRU_SEG = 8
CTX_PITCH = 36
LAT_PITCH = 260
MOD_ROWS = 24
VMEM_LIMIT = 48 * 1024 * 1024
LRU_VMEM_LIMIT = 56 * 1024 * 1024


def _layer_norm(t, g, b):
    mu = jnp.mean(t, axis=-1, keepdims=True)
    d = t - mu
    var = jnp.mean(d * d, axis=-1, keepdims=True)
    return d * lax.rsqrt(var + LN_EPS) * g + b


def _ada_kernel(c_ref, w_ref, b_ref, o_ref):
    h = jax.nn.silu(c_ref[...]).astype(BF16)
    o_ref[...] = jnp.dot(h, w_ref[...].astype(BF16), preferred_element_type=F32) + b_ref[...]


def _ada_call(cc, w_ada, b_ada):
    n_col = 3 * D_MODEL // D_MODEL
    return pl.pallas_call(
        _ada_kernel,
        grid=(DEPTH, n_col),
        in_specs=[
            pl.BlockSpec((MOD_ROWS, D_MODEL), lambda l, j: (0, 0)),
            pl.BlockSpec((None, D_MODEL, D_MODEL), lambda l, j: (l, 0, j)),
            pl.BlockSpec((None, 1, D_MODEL), lambda l, j: (l, 0, j)),
        ],
        out_specs=pl.BlockSpec((None, MOD_ROWS, D_MODEL), lambda l, j: (l, 0, j)),
        out_shape=jax.ShapeDtypeStruct((DEPTH, MOD_ROWS, 3 * D_MODEL), F32),
        compiler_params=pltpu.CompilerParams(
            dimension_semantics=("arbitrary", "arbitrary"), vmem_limit_bytes=VMEM_LIMIT),
        name="adaln",
    )(cc, w_ada, b_ada.reshape(DEPTH, 1, 3 * D_MODEL))


def _head_sumsq(t, obd):
    sq = t * t
    hi = sq.astype(BF16)
    lo = (sq - hi.astype(F32)).astype(BF16)
    cols = []
    for c in range(t.shape[1] // LANES):
        sl = slice(c * LANES, (c + 1) * LANES)
        cols.append(jnp.dot(jnp.concatenate([hi[:, sl], lo[:, sl]], axis=1), obd,
                            preferred_element_type=F32))
    return cols[0] if len(cols) == 1 else jnp.concatenate(cols, axis=1)


def _rope(t, cos, sin, first_half):
    cols = []
    for c in range(t.shape[1] // LANES):
        xc = t[:, c * LANES:(c + 1) * LANES]
        up = pltpu.roll(xc, LANES - HEAD_DIM // 4, 1)
        dn = pltpu.roll(xc, HEAD_DIM // 4, 1)
        cols.append(xc * cos + jnp.where(first_half, up, dn) * sin)
    return cols[0] if len(cols) == 1 else jnp.concatenate(cols, axis=1)


def _silu(t):
    return t * _sigmoid(t)


def _lru_coefficients(w, cw_ref, cb_ref, wg_ref, bg_ref, lam_ref):
    pad = SUBLANES
    win = w.shape[0]
    n = win - 2 * pad
    cw = cw_ref[...]
    xr = cb_ref[...] + cw[2:3] * w[pad:pad + n]
    xr = xr + cw[0:1] * pltpu.roll(w, 2, 0)[pad:pad + n]
    xr = xr + cw[1:2] * pltpu.roll(w, 1, 0)[pad:pad + n]
    xr = xr + cw[3:4] * pltpu.roll(w, win - 1, 0)[pad:pad + n]
    g = jnp.dot(xr.astype(BF16), wg_ref[...], preferred_element_type=F32) + bg_ref[...]
    half_x = 0.5 * xr
    out = []
    for d in range(N_DIR):
        nl = -lam_ref[d:d + 1, :]
        softplus = jnp.maximum(nl, 0.0) + jnp.log1p(jnp.exp(-jnp.abs(nl)))
        half_c_sp = (0.5 * LRU_C) * softplus
        t_r = jnp.tanh(0.5 * g[:, 2 * d * D_LRU:(2 * d + 1) * D_LRU])
        t_i = jnp.tanh(0.5 * g[:, (2 * d + 1) * D_LRU:(2 * d + 2) * D_LRU])
        neg_log_a = half_c_sp * t_r + half_c_sp
        a = jnp.exp2(neg_log_a * (-LOG2E))
        y = jnp.tanh(neg_log_a) * (1.0 + a * a)
        root = jnp.where(y > 0.0, y * lax.rsqrt(y), 0.0)
        out.append((a, root * (half_x * t_i + half_x)))
    return out


def _in_epilogue(z_ref, z_next_ref, seg_start, seg_end, carry_scr,
                 ang_ref, anb_ref, wcat_ref, bs_ref, qg_ref, kg_ref, cos_ref, sin_ref, obd_ref,
                 cw_ref, cb_ref, wg_ref, bg_ref, lam_ref,
                 oa_ref, qt_ref, k_ref, va_ref, gb_ref, gr_ref, a_ref, b_ref):
    rx = z_ref[:, OFF_RX:OFF_RX + D_LRU]
    head = jnp.where(seg_start, 0.0, carry_scr[...])
    tail = jnp.where(seg_end, 0.0, z_next_ref[0:SUBLANES, OFF_RX:OFF_RX + D_LRU])
    carry_scr[...] = rx[TOK_TILE - SUBLANES:]
    coeff = _lru_coefficients(jnp.concatenate([head, rx, tail], axis=0),
                              cw_ref, cb_ref, wg_ref, bg_ref, lam_ref)
    for d, (a, b) in enumerate(coeff):
        for sl in range(D_LRU // LANES):
            a_ref[d, sl] = a[:, sl * LANES:(sl + 1) * LANES]
            b_ref[d, sl] = b[:, sl * LANES:(sl + 1) * LANES]
    yield

    u = jax.nn.gelu(z_ref[:, OFF_AU:OFF_AU + D_CHUNK])
    vn = _layer_norm(jax.nn.gelu(z_ref[:, OFF_AV:OFF_AV + D_CHUNK]), ang_ref[...], anb_ref[...])
    yield
    ga = _silu(z_ref[:, OFF_AG:OFF_AG + D_CHUNK])
    grp = lax.broadcasted_iota(jnp.int32, (1, D_CHUNK), 1) // A_GDIM
    for ch in range(TOK_TILE // CHUNK):
        rs = slice(ch * CHUNK, (ch + 1) * CHUNK)
        vc = vn[rs]
        vstack = jnp.concatenate(
            [jnp.where(grp == g, vc, 0.0) for g in range(A_GROUPS)], axis=0).astype(BF16)
        s = jnp.dot(wcat_ref[...], vstack, preferred_element_type=F32) + bs_ref[...]
        oa_ref[rs, :] = (u[rs] * s * ga[rs]).astype(BF16)
    yield

    obd = obd_ref[...]
    cos = cos_ref[...]
    sin = sin_ref[...]
    lane = lax.broadcasted_iota(jnp.int32, (1, LANES), 1)
    first_half = (lane % (HEAD_DIM // 2)) < (HEAD_DIM // 4)
    q = z_ref[:, OFF_Q:OFF_Q + D_ATTN]
    q = q * lax.rsqrt(_head_sumsq(q, obd) * (1.0 / HEAD_DIM) + RMS_EPS) * qg_ref[...]
    qt_ref[...] = _rope(q, cos, sin, first_half).T.astype(BF16)
    yield
    k = z_ref[:, OFF_K:OFF_K + D_KV]
    k = k * lax.rsqrt(_head_sumsq(k, obd) * (1.0 / HEAD_DIM) + RMS_EPS) * kg_ref[...]
    k_ref[...] = _rope(k, cos, sin, first_half).astype(BF16)
    vt = z_ref[:, OFF_V:OFF_V + D_KV].T.astype(BF16)
    ones = jnp.ones((HEAD_DIM, TOK_TILE), BF16)
    va_ref[0, 0] = jnp.concatenate([vt[0:HEAD_DIM], ones], axis=0)
    va_ref[1, 0] = jnp.concatenate([ones, vt[HEAD_DIM:2 * HEAD_DIM]], axis=0)
    yield
    gb_ref[...] = _silu(z_ref[:, OFF_BG:OFF_BG + D_ATTN]).astype(BF16)
    yield

    gr_ref[...] = _silu(z_ref[:, OFF_RG:OFF_RG + D_LRU]).astype(BF16)


def _stream_tile(x_ref, ctx_ref, is_ctx, rows=slice(None)):
    if ctx_ref is None:
        return x_ref[rows, :]
    return jnp.where(is_ctx, ctx_ref[rows, :], x_ref[rows, :])


def _in_kernel(*refs, nt, split):
    if split:
        x_ref, ctx_ref, mod_ref, w_ref, *rest = refs
    else:
        x_ref, mod_ref, w_ref, *rest = refs
        ctx_ref = None
    *epi_refs, z0_scr, z1_scr, carry_scr = rest
    i = pl.program_id(0)
    n_tile = pl.num_programs(0) - 1
    t_prev = jnp.maximum(i - 1, 0) % nt
    seg_start = jnp.logical_or(t_prev == 0, t_prev == CTX_LEN // TOK_TILE)
    seg_end = jnp.logical_or(t_prev == CTX_LEN // TOK_TILE - 1, t_prev == nt - 1)

    @pl.when(i == 0)
    def _():
        z1_scr[...] = jnp.zeros(z1_scr.shape, F32)
        carry_scr[...] = jnp.zeros(carry_scr.shape, F32)

    def project(z_new):
        mod = mod_ref[...]
        shift = mod[:, 0:D_MODEL]
        scale = mod[:, D_MODEL:2 * D_MODEL]
        xt = _stream_tile(x_ref, ctx_ref, jnp.minimum(i, n_tile - 1) % nt == 0)
        xm = (xt * (1.0 + scale) + shift).astype(BF16)
        n_col = D_IN // PROJ_COLS
        for c in [OFF_RX // PROJ_COLS] + [c for c in range(n_col) if c != OFF_RX // PROJ_COLS]:
            cs = slice(c * PROJ_COLS, (c + 1) * PROJ_COLS)
            z_new[:, cs] = jnp.dot(xm, w_ref[:, cs], preferred_element_type=F32)
            yield

    def step(z_new, z_prev):
        stages = [project(z_new), _in_epilogue(z_prev, z_new, seg_start, seg_end, carry_scr, *epi_refs)]
        while stages:
            for st in list(stages):
                if next(st, StopIteration) is StopIteration:
                    stages.remove(st)

    @pl.when(i % 2 == 0)
    def _():
        step(z0_scr, z1_scr)

    @pl.when(i % 2 == 1)
    def _():
        step(z1_scr, z0_scr)


def _in_call(stream, ss, w_in, ang, anb, wcat, bs_full, qg, kg, cos_t, sin_t, obd, cw, cb, wg, bg, lam):
    split = isinstance(stream, tuple)
    bn, lt = stream[0].shape[:2] if split else stream.shape[:2]
    lt += CTX_LEN if split else 0
    nt = lt // TOK_TILE
    n_tile = bn * nt
    cur = lambda i: jnp.minimum(i, n_tile - 1)
    prev = lambda i: jnp.maximum(i - 1, 0)
    tok = lambda width: pl.BlockSpec((None, TOK_TILE, width), lambda i: (prev(i) // nt, prev(i) % nt, 0))
    full = lambda a: pl.BlockSpec(a.shape, lambda i: (0,) * a.ndim)
    slabs = D_LRU // LANES
    coeff_spec = pl.BlockSpec((None, N_DIR, slabs, TOK_TILE, LANES), lambda i: (prev(i) // nt, 0, 0, prev(i) % nt, 0))
    coeff_shape = jax.ShapeDtypeStruct((bn, N_DIR, slabs, lt, LANES), F32)
    if split:
        stream_specs = [
            pl.BlockSpec((None, TOK_TILE, D_MODEL), lambda i: (cur(i) // nt, jnp.maximum(cur(i) % nt - 1, 0), 0)),
            pl.BlockSpec((None, CTX_LEN, D_MODEL), lambda i: (cur(i) // nt, 0, 0)),
        ]
    else:
        stream = (stream,)
        stream_specs = [pl.BlockSpec((None, TOK_TILE, D_MODEL), lambda i: (cur(i) // nt, cur(i) % nt, 0))]
    return pl.pallas_call(
        functools.partial(_in_kernel, nt=nt, split=split),
        grid=(n_tile + 1,),
        in_specs=stream_specs + [
            pl.BlockSpec((None, None, 1, 3 * D_MODEL),
                         lambda i: (cur(i) // nt, jnp.minimum(cur(i) % nt, 1), 0, 0)),
            full(w_in), full(ang), full(anb), full(wcat), full(bs_full), full(qg), full(kg),
            pl.BlockSpec((TOK_TILE, LANES), lambda i: (prev(i) % nt, 0)),
            pl.BlockSpec((TOK_TILE, LANES), lambda i: (prev(i) % nt, 0)),
            full(obd), full(cw), full(cb), full(wg), full(bg), full(lam),
        ],
        out_specs=[
            tok(D_CHUNK),
            pl.BlockSpec((None, None, D_ATTN, TOK_TILE), lambda i: (prev(i) // nt, prev(i) % nt, 0, 0)),
            tok(D_KV),
            pl.BlockSpec((None, N_KV_HEADS, 1, D_KV, TOK_TILE),
                         lambda i: (prev(i) // nt, 0, prev(i) % nt, 0, 0)),
            tok(D_ATTN), tok(D_LRU), coeff_spec, coeff_spec,
        ],
        out_shape=[
            jax.ShapeDtypeStruct((bn, lt, D_CHUNK), BF16),
            jax.ShapeDtypeStruct((bn, nt, D_ATTN, TOK_TILE), BF16),
            jax.ShapeDtypeStruct((bn, lt, D_KV), BF16),
            jax.ShapeDtypeStruct((bn, N_KV_HEADS, nt, D_KV, TOK_TILE), BF16),
            jax.ShapeDtypeStruct((bn, lt, D_ATTN), BF16),
            jax.ShapeDtypeStruct((bn, lt, D_LRU), BF16),
            coeff_shape, coeff_shape,
        ],
        scratch_shapes=[pltpu.VMEM((TOK_TILE, D_IN), F32), pltpu.VMEM((TOK_TILE, D_IN), F32),
                        pltpu.VMEM((SUBLANES, D_LRU), F32)],
        compiler_params=pltpu.CompilerParams(
            dimension_semantics=("arbitrary",), vmem_limit_bytes=VMEM_LIMIT),
        name="in_proj",
    )(*stream, ss, w_in, ang, anb, wcat, bs_full, qg, kg, cos_t, sin_t, obd, cw, cb, wg, bg, lam)


def _sigmoid(t):
    return 0.5 * jnp.tanh(0.5 * t) + 0.5


def _lru_kernel(a_ref, b_ref, gr_ref, o_ref, a_scr, b_scr, p_scr, h_scr):
    lt = gr_ref.shape[0]
    n_lat = lt - CTX_LEN
    n_chunk = lt // LRU_T
    n_ctx_chunk = CTX_LEN // LRU_T
    ctx_rows = LRU_SEG * CTX_PITCH
    lat_rows = LRU_SEG * LAT_PITCH
    assert CTX_LEN <= ctx_rows and n_lat <= lat_rows and (ctx_rows - CTX_LEN) % SUBLANES == 0
    regions = ((0, CTX_PITCH, CTX_LEN, 0), (ctx_rows, LAT_PITCH, n_lat, CTX_LEN))
    slabs = D_LRU // LANES

    for off, pitch, rows, tok0 in regions:
        n_pad = LRU_SEG * pitch - rows
        for d in range(N_DIR):
            for sl in range(slabs):
                a_scr[d, sl, off:off + rows, :] = a_ref[d, sl, tok0:tok0 + rows, :]
                b_scr[d, sl, off:off + rows, :] = b_ref[d, sl, tok0:tok0 + rows, :]
                a_scr[d, sl, off + rows:off + rows + n_pad, :] = jnp.ones((n_pad, LANES), F32)
                b_scr[d, sl, off + rows:off + rows + n_pad, :] = jnp.zeros((n_pad, LANES), F32)

    def sweep(off, pitch):
        def body(i, state):
            new = []
            for d in range(N_DIR):
                t = i if d == 0 else pitch - 1 - i
                idx = pl.ds(off + t, LRU_SEG, stride=pitch)
                for sl in range(slabs):
                    h, p = state[2 * (d * slabs + sl)], state[2 * (d * slabs + sl) + 1]
                    a = a_scr[d, sl, idx, :]
                    h = a * h + b_scr[d, sl, idx, :]
                    p = a * p
                    h_scr[d, sl, idx, :] = h
                    p_scr[d, sl, idx, :] = p
                    new += [h, p]
            return tuple(new)

        ident = (jnp.zeros((LRU_SEG, LANES), F32), jnp.ones((LRU_SEG, LANES), F32)) * (N_DIR * slabs)
        lax.fori_loop(0, pitch, body, ident, unroll=4)

    for off, pitch, _, _ in regions:
        sweep(off, pitch)

    def chain(d, off, pitch, init):
        carries = [None] * LRU_SEG
        c = init
        for j in (range(LRU_SEG) if d == 0 else reversed(range(LRU_SEG))):
            carries[j] = c
            last = off + j * pitch + (pitch - 1 if d == 0 else 0)
            c = [h_scr[d, sl, last:last + 1, :] + p_scr[d, sl, last:last + 1, :] * c[sl]
                 for sl in range(slabs)]
        return carries, c

    zero = [jnp.zeros((1, LANES), F32)] * slabs
    carries = []
    for d in range(N_DIR):
        c_ctx, fin = chain(d, regions[0][0], regions[0][1], zero)
        c_lat, _ = chain(d, regions[1][0], regions[1][1], fin)
        carries.append((c_ctx, c_lat))

    row = lax.broadcasted_iota(jnp.int32, (LRU_T, 1), 0)
    for ci in range(n_chunk):
        reg = 0 if ci < n_ctx_chunk else 1
        off, pitch, _, tok0 = regions[reg]
        r0 = ci * LRU_T - tok0
        j_lo, j_hi = r0 // pitch, (r0 + LRU_T - 1) // pitch
        for sl in range(slabs):
            rows = slice(off + r0, off + r0 + LRU_T)
            h = None
            for d in range(N_DIR):
                seg_c = carries[d][reg]
                c = seg_c[j_hi][sl]
                for j in range(j_hi - 1, j_lo - 1, -1):
                    c = jnp.where(row < (j + 1) * pitch - r0, seg_c[j][sl], c)
                hd = h_scr[d, sl, rows, :] + p_scr[d, sl, rows, :] * c
                h = hd if h is None else h + hd
            cols = slice(sl * LANES, (sl + 1) * LANES)
            gate = gr_ref[ci * LRU_T:(ci + 1) * LRU_T, cols].astype(F32)
            o_ref[ci * LRU_T:(ci + 1) * LRU_T, cols] = (h * gate).astype(BF16)


def _lru_call(a, b, gr):
    bn, lt, _ = gr.shape
    seq = pl.BlockSpec((None, lt, D_LRU), lambda i: (i, 0, 0))
    coeff = pl.BlockSpec((None,) + a.shape[1:], lambda i: (i, 0, 0, 0, 0))
    scan_rows = LRU_SEG * (CTX_PITCH + LAT_PITCH)
    return pl.pallas_call(
        _lru_kernel,
        grid=(bn,),
        in_specs=[coeff, coeff, seq],
        out_specs=seq,
        out_shape=jax.ShapeDtypeStruct((bn, lt, D_LRU), BF16),
        scratch_shapes=[pltpu.VMEM((N_DIR, D_LRU // LANES, scan_rows, LANES), F32)] * 4,
        compiler_params=pltpu.CompilerParams(
            dimension_semantics=("parallel",), vmem_limit_bytes=LRU_VMEM_LIMIT),
        name="rglru",
    )(a, b, gr)


def _attend(qt_ref, k_ref, va_ref, att_scr, rhs_scr, s_scr, p_scr, ot_scr, nk, bound):
    pair = LANES // HEAD_DIM
    width = GQA_GROUP * Q_TILE
    n_chunk = nk // KEY_CHUNK
    units = [(qb, g) for qb in range(qt_ref.shape[1] // Q_TILE) for g in range(N_KV_HEADS)]
    n_unit = len(units)
    zeros = jnp.zeros((HEAD_DIM, width), BF16)

    def stage_rhs(u):
        qb, g = units[u]
        h0 = g * GQA_GROUP
        qt = jnp.concatenate(
            [qt_ref[(h0 + j) * HEAD_DIM:(h0 + j + 1) * HEAD_DIM, qb * Q_TILE:(qb + 1) * Q_TILE]
             for j in range(GQA_GROUP)], axis=1)
        rhs_scr[u % 2] = jnp.concatenate([qt, zeros] if g == 0 else [zeros, qt], axis=0)

    def score_chunk(u, c, mx):
        ks = pl.ds(pl.multiple_of(c * KEY_CHUNK, KEY_CHUNK), KEY_CHUNK)
        s = jnp.dot(k_ref[ks, :], rhs_scr[u % 2], preferred_element_type=F32)
        s_scr[u % 2, c] = s
        for i in range(KEY_CHUNK // SUBLANES):
            mx = jnp.maximum(mx, s[i * SUBLANES:(i + 1) * SUBLANES])
        return mx

    def exp_chunk(u, c, m):
        p_scr[u % 2, c] = jnp.exp2(s_scr[u % 2, c] - m).astype(BF16)

    def pv_chunk(u, c):
        g = units[u][1]
        ot_scr[...] += jnp.dot(va_ref[g, c], p_scr[u % 2, c], preferred_element_type=F32)

    def finish(u):
        qb, g = units[u]
        ot = ot_scr[...]
        if g == 0:
            o, l = ot[0:HEAD_DIM], ot[HEAD_DIM:HEAD_DIM + 1]
        else:
            o, l = ot[HEAD_DIM:2 * HEAD_DIM], ot[0:1]
        o = o * (1.0 / l)
        for rp in range(GQA_GROUP // pair):
            stacked = jnp.concatenate(
                [o[:, (rp * pair + j) * Q_TILE:(rp * pair + j + 1) * Q_TILE] for j in range(pair)], axis=0)
            col = (g * GQA_GROUP + rp * pair) // pair
            att_scr[qb * Q_TILE:(qb + 1) * Q_TILE, col * LANES:(col + 1) * LANES] = stacked.T

    def score_exp_chunk(u, c):
        ks = pl.ds(pl.multiple_of(c * KEY_CHUNK, KEY_CHUNK), KEY_CHUNK)
        s = jnp.dot(k_ref[ks, :], rhs_scr[u % 2], preferred_element_type=F32)
        p_scr[u % 2, c] = jnp.exp2(s - bound).astype(BF16)

    if bound is not None:
        for t in range(n_unit + 1):
            ua, uc = t, t - 1
            do_a, do_c = ua < n_unit, uc >= 0
            if do_a:
                stage_rhs(ua)
            if do_c:
                ot_scr[...] = jnp.zeros(ot_scr.shape, F32)

            def fast_body(c, carry, ua=ua, uc=uc, do_a=do_a, do_c=do_c):
                if do_a:
                    score_exp_chunk(ua, c)
                if do_c:
                    pv_chunk(uc, c)
                return carry

            lax.fori_loop(0, n_chunk, fast_body, 0, unroll=min(PIPE_UNROLL, n_chunk))
            if do_c:
                finish(uc)
        return

    col_max = {}
    for t in range(n_unit + 2):
        ua, ub, uc = t, t - 1, t - 2
        do_a, do_b, do_c = ua < n_unit, 0 <= ub < n_unit, 0 <= uc < n_unit
        if do_a:
            stage_rhs(ua)
        if do_c:
            ot_scr[...] = jnp.zeros(ot_scr.shape, F32)

        def body(c, mx, ua=ua, ub=ub, uc=uc, do_a=do_a, do_b=do_b, do_c=do_c):
            if do_a:
                mx = score_chunk(ua, c, mx)
            if do_b:
                exp_chunk(ub, c, col_max[ub])
            if do_c:
                pv_chunk(uc, c)
            return mx

        mx = lax.fori_loop(0, n_chunk, body, jnp.full((SUBLANES, width), -jnp.inf, F32),
                           unroll=min(PIPE_UNROLL, n_chunk))
        if do_a:
            col_max[ua] = jnp.max(mx, axis=0, keepdims=True)
        if do_c:
            finish(uc)


def _attn_kernel(flag_ref, bnd_ref, qt_ref, k_ref, va_ref, oa_ref, gb_ref, ol_ref, *refs,
                 q_off, with_ctx, split):
    if split:
        x_ref, ctx_ref, *refs = refs
    else:
        x_ref, *refs = refs
        ctx_ref = None
    mod_ref, wo_ref, lng_ref, lnb_ref, o_ref, att_scr, rhs_scr, s_scr, p_scr, ot_scr = refs
    nk_all = k_ref.shape[0]
    scratch = (att_scr, rhs_scr, s_scr, p_scr, ot_scr)
    bounded = flag_ref[0] == 1
    is_ctx = (pl.program_id(1) + q_off) < (CTX_LEN // Q_STEP)

    for ctx_step, nk in ((True, CTX_LEN), (False, nk_all)):
        if ctx_step and not with_ctx:
            continue
        here = (is_ctx if ctx_step else jnp.logical_not(is_ctx)) if with_ctx else True

        @pl.when(jnp.logical_and(here, bounded))
        def _():
            _attend(qt_ref, k_ref, va_ref, *scratch, nk, bnd_ref[...])

        @pl.when(jnp.logical_and(here, jnp.logical_not(bounded)))
        def _():
            _attend(qt_ref, k_ref, va_ref, *scratch, nk, None)

    att = (att_scr[...] * gb_ref[...].astype(F32)).astype(BF16)
    cat = jnp.concatenate([oa_ref[...], att, ol_ref[...]], axis=1)
    y = jnp.dot(cat, wo_ref[...], preferred_element_type=F32)
    gate = mod_ref[:, 2 * D_MODEL:3 * D_MODEL]
    xt = _stream_tile(x_ref, ctx_ref, is_ctx)
    o_ref[...] = _layer_norm(ALPHA * xt + gate * y, lng_ref[...], lnb_ref[...])


def _attn_call(flag, bnd, qt, k, va, oa, gb, ol, stream, ss, w_o, lng, lnb, last):
    split = isinstance(stream, tuple)
    assert not (split and last) and Q_STEP == TOK_TILE
    bn, lt, _ = oa.shape
    n_ctx_step = CTX_LEN // Q_STEP
    q_off = n_ctx_step if last else 0
    nq = lt // Q_STEP - q_off
    tok = lambda width: pl.BlockSpec((None, Q_STEP, width), lambda b, i: (b, i + q_off, 0))
    full = lambda a: pl.BlockSpec(a.shape, lambda b, i: (0,) * a.ndim)
    if split:
        stream_specs = [
            pl.BlockSpec((None, Q_STEP, D_MODEL), lambda b, i: (b, jnp.maximum(i - 1, 0), 0)),
            pl.BlockSpec((None, CTX_LEN, D_MODEL), lambda b, i: (b, 0, 0)),
        ]
    else:
        stream = (stream,)
        stream_specs = [tok(D_MODEL)]
    return pl.pallas_call(
        functools.partial(_attn_kernel, q_off=q_off, with_ctx=not last, split=split),
        grid=(bn, nq),
        in_specs=[
            pl.BlockSpec(memory_space=pltpu.SMEM),
            full(bnd),
            pl.BlockSpec((None, None, D_ATTN, Q_STEP), lambda b, i: (b, i + q_off, 0, 0)),
            pl.BlockSpec((None, lt, D_KV), lambda b, i: (b, 0, 0)),
            pl.BlockSpec((None, N_KV_HEADS, lt // KEY_CHUNK, D_KV, KEY_CHUNK), lambda b, i: (b, 0, 0, 0, 0)),
            tok(D_CHUNK), tok(D_ATTN), tok(D_LRU),
        ] + stream_specs + [
            pl.BlockSpec((None, None, 1, 3 * D_MODEL),
                         lambda b, i: (b, jnp.minimum((i + q_off) // n_ctx_step, 1), 0, 0)),
            full(w_o), full(lng), full(lnb),
        ],
        out_specs=pl.BlockSpec((None, Q_STEP, D_MODEL), lambda b, i: (b, i, 0)),
        out_shape=jax.ShapeDtypeStruct((bn, nq * Q_STEP, D_MODEL), F32),
        scratch_shapes=[
            pltpu.VMEM((Q_STEP, D_ATTN), F32),
            pltpu.VMEM((2, D_KV, GQA_GROUP * Q_TILE), BF16),
            pltpu.VMEM((2, lt // KEY_CHUNK, KEY_CHUNK, GQA_GROUP * Q_TILE), F32),
            pltpu.VMEM((2, lt // KEY_CHUNK, KEY_CHUNK, GQA_GROUP * Q_TILE), BF16),
            pltpu.VMEM((D_KV, GQA_GROUP * Q_TILE), F32),
        ],
        compiler_params=pltpu.CompilerParams(
            dimension_semantics=("parallel", "arbitrary"), vmem_limit_bytes=VMEM_LIMIT),
        name="attn_merge",
    )(flag, bnd, qt, k, va, oa, gb, ol, *stream, ss, w_o, lng, lnb)


def _rope_tables(n_lat):
    nf = HEAD_DIM // 4
    t = jnp.arange(n_lat, dtype=jnp.int32)
    pos = jnp.stack([t // GRID_W, t % GRID_W], axis=1).astype(F32)
    inv = ROPE_THETA ** (-jnp.arange(nf, dtype=F32) / nf)
    d = np.arange(HEAD_DIM)
    ang = pos[:, d // (HEAD_DIM // 2)] * inv[d % nf]
    sign = jnp.asarray(np.where((d % (HEAD_DIM // 2)) < nf, -1.0, 1.0), F32)
    cos = jnp.concatenate([jnp.ones((CTX_LEN, HEAD_DIM), F32), jnp.cos(ang)], axis=0)
    sin = jnp.concatenate([jnp.zeros((CTX_LEN, HEAD_DIM), F32), jnp.sin(ang) * sign], axis=0)
    reps = LANES // HEAD_DIM
    return jnp.tile(cos, (1, reps)), jnp.tile(sin, (1, reps))


def _block_diag(w):
    nb, n, _ = w.shape
    eye = jnp.eye(nb, dtype=w.dtype)
    return (eye[:, None, :, None] * w[:, :, None, :]).reshape(nb * n, nb * n)


def kernel(x, c, ctx, c_ctx, w_ada, b_ada, w_in, a_norm_g, a_norm_b, a_ws, a_bs, q_norm_g, k_norm_g,
           conv_w, conv_b, lru_wr, lru_br, lru_wi, lru_bi, lru_lam, w_o, ln_g, ln_b):
    bn, n_lat, _ = x.shape
    assert ctx.shape[1] == CTX_LEN and n_lat % TOK_TILE == 0 and bn + 1 <= MOD_ROWS
    xa = (x, ctx)

    cc = jnp.zeros((MOD_ROWS, D_MODEL), F32).at[:bn].set(c).at[bn].set(c_ctx)
    mod = _ada_call(cc, w_ada, b_ada)
    ctx_mod = jnp.broadcast_to(mod[:, bn:bn + 1], (DEPTH, bn, 3 * D_MODEL))
    ss = jnp.stack([ctx_mod, mod[:, :bn]], axis=2).reshape(DEPTH, bn, 2, 1, 3 * D_MODEL)

    cos_t, sin_t = _rope_tables(n_lat)
    head = np.arange(LANES) // HEAD_DIM
    obd = jnp.asarray(np.tile(head[:, None] == head[None, :], (2, 1)), BF16)

    w_in_b = w_in.astype(BF16)
    w_o_b = w_o.astype(BF16)
    for l in range(DEPTH):
        wcat = jnp.concatenate([a_ws[l, g] for g in range(A_GROUPS)], axis=1).astype(BF16)
        bs_full = jnp.repeat(a_bs[l].T, A_GDIM, axis=1)
        qg = (jnp.tile(q_norm_g[l], N_HEADS) * (HEAD_DIM ** -0.5 * LOG2E)).reshape(1, D_ATTN)
        kg = jnp.tile(k_norm_g[l], N_KV_HEADS).reshape(1, D_KV)
        wg = jnp.concatenate(
            [m for d in range(N_DIR) for m in (_block_diag(lru_wr[l, d]), _block_diag(lru_wi[l, d]))],
            axis=1).astype(BF16)
        bg = jnp.concatenate(
            [m for d in range(N_DIR) for m in (lru_br[l, d], lru_bi[l, d])]).reshape(1, 4 * D_LRU)
        oa, qt, k, va, gb, gr, lru_a, lru_b = _in_call(
            xa, ss[l], w_in_b[l], a_norm_g[l].reshape(1, D_CHUNK), a_norm_b[l].reshape(1, D_CHUNK),
            wcat, bs_full, qg, kg, cos_t, sin_t, obd,
            conv_w[l], conv_b[l].reshape(1, D_LRU), wg, bg, lru_lam[l])
        ol = _lru_call(lru_a, lru_b, gr)

        s_bound = (SCORE_BOUND_SLACK * HEAD_DIM ** 0.5 * LOG2E
                   * jnp.max(jnp.abs(q_norm_g[l])) * jnp.max(jnp.abs(k_norm_g[l])))
        flag = (s_bound <= MAX_SAFE_SCORE_BOUND).astype(jnp.int32).reshape(1)
        bnd = jnp.full((1, GQA_GROUP * Q_TILE), s_bound, F32)
        xa = _attn_call(flag, bnd, qt, k, va, oa, gb, ol, xa, ss[l], w_o_b[l], ln_g[l].reshape(1, D_MODEL),
                        ln_b[l].reshape(1, D_MODEL), last=(l == DEPTH - 1))
    return xa
```

```python
import functools

import jax
import jax.numpy as jnp
import numpy as np
from jax import lax
from jax.experimental import pallas as pl
from jax.experimental.pallas import tpu as pltpu

F32 = jnp.float32
BF16 = jnp.bfloat16

D_MODEL = 1024
DEPTH = 4
GRID_W = 64
CTX_LEN = 256
D_CHUNK = D_MODEL // 4
D_ATTN = D_MODEL // 2
D_LRU = D_MODEL // 4
CHUNK = 128
A_GROUPS = 4
A_GDIM = D_CHUNK // A_GROUPS
HEAD_DIM = 64
N_HEADS = D_ATTN // HEAD_DIM
N_KV_HEADS = N_HEADS // 4
GQA_GROUP = N_HEADS // N_KV_HEADS
D_KV = N_KV_HEADS * HEAD_DIM
ROPE_THETA = 10000.0
LRU_BLOCKS = 4
LRU_BDIM = D_LRU // LRU_BLOCKS
CONV_W = 4
LRU_C = 8.0
N_DIR = 2
D_IN = 3 * D_CHUNK + 2 * D_ATTN + 2 * D_KV + 2 * D_LRU
ALPHA = (2.0 * DEPTH) ** 0.25
LN_EPS = 1e-6
RMS_EPS = 1e-6
LOG2E = 1.4426950408889634
SCORE_BOUND_SLACK = 1.02
MAX_SAFE_SCORE_BOUND = 40.0

OFF_AU, OFF_AV, OFF_AG = 0, D_CHUNK, 2 * D_CHUNK
OFF_Q = 3 * D_CHUNK
OFF_K = OFF_Q + D_ATTN
OFF_V = OFF_K + D_KV
OFF_BG = OFF_V + D_KV
OFF_RX = OFF_BG + D_ATTN
OFF_RG = OFF_RX + D_LRU

LANES = 128
SUBLANES = 8
TOK_TILE = 256
PROJ_COLS = 512
Q_TILE = 128
Q_STEP = 256
PIPE_UNROLL = 9
KEY_CHUNK = 256
LRU_T = 128
LRU_SEG = 8
CTX_PITCH = 36
LAT_PITCH = 260
MOD_ROWS = 24
VMEM_LIMIT = 48 * 1024 * 1024
LRU_VMEM_LIMIT = 56 * 1024 * 1024


def _layer_norm(t, g, b):
    mu = jnp.mean(t, axis=-1, keepdims=True)
    d = t - mu
    var = jnp.mean(d * d, axis=-1, keepdims=True)
    return d * lax.rsqrt(var + LN_EPS) * g + b


def _ada_kernel(c_ref, w_ref, b_ref, o_ref):
    h = jax.nn.silu(c_ref[...]).astype(BF16)
    o_ref[...] = jnp.dot(h, w_ref[...].astype(BF16), preferred_element_type=F32) + b_ref[...]


def _ada_call(cc, w_ada, b_ada):
    n_col = 3 * D_MODEL // D_MODEL
    return pl.pallas_call(
        _ada_kernel,
        grid=(DEPTH, n_col),
        in_specs=[
            pl.BlockSpec((MOD_ROWS, D_MODEL), lambda l, j: (0, 0)),
            pl.BlockSpec((None, D_MODEL, D_MODEL), lambda l, j: (l, 0, j)),
            pl.BlockSpec((None, 1, D_MODEL), lambda l, j: (l, 0, j)),
        ],
        out_specs=pl.BlockSpec((None, MOD_ROWS, D_MODEL), lambda l, j: (l, 0, j)),
        out_shape=jax.ShapeDtypeStruct((DEPTH, MOD_ROWS, 3 * D_MODEL), F32),
        compiler_params=pltpu.CompilerParams(
            dimension_semantics=("arbitrary", "arbitrary"), vmem_limit_bytes=VMEM_LIMIT),
        name="adaln",
    )(cc, w_ada, b_ada.reshape(DEPTH, 1, 3 * D_MODEL))


def _head_sumsq(t, obd):
    sq = t * t
    hi = sq.astype(BF16)
    lo = (sq - hi.astype(F32)).astype(BF16)
    cols = []
    for c in range(t.shape[1] // LANES):
        sl = slice(c * LANES, (c + 1) * LANES)
        cols.append(jnp.dot(jnp.concatenate([hi[:, sl], lo[:, sl]], axis=1), obd,
                            preferred_element_type=F32))
    return cols[0] if len(cols) == 1 else jnp.concatenate(cols, axis=1)


def _rope(t, cos, sin, first_half):
    cols = []
    for c in range(t.shape[1] // LANES):
        xc = t[:, c * LANES:(c + 1) * LANES]
        up = pltpu.roll(xc, LANES - HEAD_DIM // 4, 1)
        dn = pltpu.roll(xc, HEAD_DIM // 4, 1)
        cols.append(xc * cos + jnp.where(first_half, up, dn) * sin)
    return cols[0] if len(cols) == 1 else jnp.concatenate(cols, axis=1)


def _silu(t):
    return t * _sigmoid(t)


def _lru_coefficients(w, cw_ref, cb_ref, wg_ref, bg_ref, lam_ref):
    pad = SUBLANES
    win = w.shape[0]
    n = win - 2 * pad
    cw = cw_ref[...]
    xr = cb_ref[...] + cw[2:3] * w[pad:pad + n]
    xr = xr + cw[0:1] * pltpu.roll(w, 2, 0)[pad:pad + n]
    xr = xr + cw[1:2] * pltpu.roll(w, 1, 0)[pad:pad + n]
    xr = xr + cw[3:4] * pltpu.roll(w, win - 1, 0)[pad:pad + n]
    g = jnp.dot(xr.astype(BF16), wg_ref[...], preferred_element_type=F32) + bg_ref[...]
    half_x = 0.5 * xr
    out = []
    for d in range(N_DIR):
        nl = -lam_ref[d:d + 1, :]
        softplus = jnp.maximum(nl, 0.0) + jnp.log1p(jnp.exp(-jnp.abs(nl)))
        half_c_sp = (0.5 * LRU_C) * softplus
        t_r = jnp.tanh(0.5 * g[:, 2 * d * D_LRU:(2 * d + 1) * D_LRU])
        t_i = jnp.tanh(0.5 * g[:, (2 * d + 1) * D_LRU:(2 * d + 2) * D_LRU])
        neg_log_a = half_c_sp * t_r + half_c_sp
        a = jnp.exp2(neg_log_a * (-LOG2E))
        y = jnp.tanh(neg_log_a) * (1.0 + a * a)
        root = jnp.where(y > 0.0, y * lax.rsqrt(y), 0.0)
        out.append((a, root * (half_x * t_i + half_x)))
    return out


def _in_epilogue(z_ref, z_next_ref, seg_start, seg_end, carry_scr,
                 ang_ref, anb_ref, wcat_ref, bs_ref, qg_ref, kg_ref, cos_ref, sin_ref, obd_ref,
                 cw_ref, cb_ref, wg_ref, bg_ref, lam_ref,
                 oa_ref, qt_ref, k_ref, va_ref, gb_ref, gr_ref, a_ref, b_ref):
    rx = z_ref[:, OFF_RX:OFF_RX + D_LRU]
    head = jnp.where(seg_start, 0.0, carry_scr[...])
    tail = jnp.where(seg_end, 0.0, z_next_ref[0:SUBLANES, OFF_RX:OFF_RX + D_LRU])
    carry_scr[...] = rx[TOK_TILE - SUBLANES:]
    coeff = _lru_coefficients(jnp.concatenate([head, rx, tail], axis=0),
                              cw_ref, cb_ref, wg_ref, bg_ref, lam_ref)
    for d, (a, b) in enumerate(coeff):
        for sl in range(D_LRU // LANES):
            a_ref[d, sl] = a[:, sl * LANES:(sl + 1) * LANES]
            b_ref[d, sl] = b[:, sl * LANES:(sl + 1) * LANES]
    yield

    u = jax.nn.gelu(z_ref[:, OFF_AU:OFF_AU + D_CHUNK])
    vn = _layer_norm(jax.nn.gelu(z_ref[:, OFF_AV:OFF_AV + D_CHUNK]), ang_ref[...], anb_ref[...])
    yield
    ga = _silu(z_ref[:, OFF_AG:OFF_AG + D_CHUNK])
    grp = lax.broadcasted_iota(jnp.int32, (1, D_CHUNK), 1) // A_GDIM
    for ch in range(TOK_TILE // CHUNK):
        rs = slice(ch * CHUNK, (ch + 1) * CHUNK)
        vc = vn[rs]
        vstack = jnp.concatenate(
            [jnp.where(grp == g, vc, 0.0) for g in range(A_GROUPS)], axis=0).astype(BF16)
        s = jnp.dot(wcat_ref[...], vstack, preferred_element_type=F32) + bs_ref[...]
        oa_ref[rs, :] = (u[rs] * s * ga[rs]).astype(BF16)
    yield

    obd = obd_ref[...]
    cos = cos_ref[...]
    sin = sin_ref[...]
    lane = lax.broadcasted_iota(jnp.int32, (1, LANES), 1)
    first_half = (lane % (HEAD_DIM // 2)) < (HEAD_DIM // 4)
    q = z_ref[:, OFF_Q:OFF_Q + D_ATTN]
    q = q * lax.rsqrt(_head_sumsq(q, obd) * (1.0 / HEAD_DIM) + RMS_EPS) * qg_ref[...]
    qt_ref[...] = _rope(q, cos, sin, first_half).T.astype(BF16)
    yield
    k = z_ref[:, OFF_K:OFF_K + D_KV]
    k = k * lax.rsqrt(_head_sumsq(k, obd) * (1.0 / HEAD_DIM) + RMS_EPS) * kg_ref[...]
    k_ref[...] = _rope(k, cos, sin, first_half).astype(BF16)
    vt = z_ref[:, OFF_V:OFF_V + D_KV].T.astype(BF16)
    ones = jnp.ones((HEAD_DIM, TOK_TILE), BF16)
    va_ref[0, 0] = jnp.concatenate([vt[0:HEAD_DIM], ones], axis=0)
    va_ref[1, 0] = jnp.concatenate([ones, vt[HEAD_DIM:2 * HEAD_DIM]], axis=0)
    yield
    gb_ref[...] = _silu(z_ref[:, OFF_BG:OFF_BG + D_ATTN]).astype(BF16)
    yield

    gr_ref[...] = _silu(z_ref[:, OFF_RG:OFF_RG + D_LRU]).astype(BF16)


def _stream_tile(x_ref, ctx_ref, is_ctx, rows=slice(None)):
    if ctx_ref is None:
        return x_ref[rows, :]
    return jnp.where(is_ctx, ctx_ref[rows, :], x_ref[rows, :])


def _in_kernel(*refs, nt, split):
    if split:
        x_ref, ctx_ref, mod_ref, w_ref, *rest = refs
    else:
        x_ref, mod_ref, w_ref, *rest = refs
        ctx_ref = None
    *epi_refs, z0_scr, z1_scr, carry_scr = rest
    i = pl.program_id(0)
    n_tile = pl.num_programs(0) - 1
    t_prev = jnp.maximum(i - 1, 0) % nt
    seg_start = jnp.logical_or(t_prev == 0, t_prev == CTX_LEN // TOK_TILE)
    seg_end = jnp.logical_or(t_prev == CTX_LEN // TOK_TILE - 1, t_prev == nt - 1)

    @pl.when(i == 0)
    def _():
        z1_scr[...] = jnp.zeros(z1_scr.shape, F32)
        carry_scr[...] = jnp.zeros(carry_scr.shape, F32)

    def project(z_new):
        mod = mod_ref[...]
        shift = mod[:, 0:D_MODEL]
        scale = mod[:, D_MODEL:2 * D_MODEL]
        xt = _stream_tile(x_ref, ctx_ref, jnp.minimum(i, n_tile - 1) % nt == 0)
        xm = (xt * (1.0 + scale) + shift).astype(BF16)
        n_col = D_IN // PROJ_COLS
        for c in [OFF_RX // PROJ_COLS] + [c for c in range(n_col) if c != OFF_RX // PROJ_COLS]:
            cs = slice(c * PROJ_COLS, (c + 1) * PROJ_COLS)
            z_new[:, cs] = jnp.dot(xm, w_ref[:, cs], preferred_element_type=F32)
            yield

    def step(z_new, z_prev):
        stages = [project(z_new), _in_epilogue(z_prev, z_new, seg_start, seg_end, carry_scr, *epi_refs)]
        while stages:
            for st in list(stages):
                if next(st, StopIteration) is StopIteration:
                    stages.remove(st)

    @pl.when(i % 2 == 0)
    def _():
        step(z0_scr, z1_scr)

    @pl.when(i % 2 == 1)
    def _():
        step(z1_scr, z0_scr)


def _in_call(stream, ss, w_in, layer, ang, anb, wcat, bs_full, qg, kg, cos_t, sin_t, obd, cw, cb, wg, bg, lam):
    split = isinstance(stream, tuple)
    bn, lt = stream[0].shape[:2] if split else stream.shape[:2]
    lt += CTX_LEN if split else 0
    nt = lt // TOK_TILE
    n_tile = bn * nt
    cur = lambda i: jnp.minimum(i, n_tile - 1)
    prev = lambda i: jnp.maximum(i - 1, 0)
    tok = lambda width: pl.BlockSpec((None, TOK_TILE, width), lambda i: (prev(i) // nt, prev(i) % nt, 0))
    full = lambda a: pl.BlockSpec(a.shape, lambda i: (0,) * a.ndim)
    slabs = D_LRU // LANES
    coeff_spec = pl.BlockSpec((None, N_DIR, slabs, TOK_TILE, LANES), lambda i: (prev(i) // nt, 0, 0, prev(i) % nt, 0))
    coeff_shape = jax.ShapeDtypeStruct((bn, N_DIR, slabs, lt, LANES), F32)
    if split:
        stream_specs = [
            pl.BlockSpec((None, TOK_TILE, D_MODEL), lambda i: (cur(i) // nt, jnp.maximum(cur(i) % nt - 1, 0), 0)),
            pl.BlockSpec((None, CTX_LEN, D_MODEL), lambda i: (cur(i) // nt, 0, 0)),
        ]
    else:
        stream = (stream,)
        stream_specs = [pl.BlockSpec((None, TOK_TILE, D_MODEL), lambda i: (cur(i) // nt, cur(i) % nt, 0))]
    return pl.pallas_call(
        functools.partial(_in_kernel, nt=nt, split=split),
        grid=(n_tile + 1,),
        in_specs=stream_specs + [
            pl.BlockSpec((None, None, 1, 3 * D_MODEL),
                         lambda i: (cur(i) // nt, jnp.minimum(cur(i) % nt, 1), 0, 0)),
            pl.BlockSpec((None,) + w_in.shape[1:], lambda i: (layer, 0, 0)),
            full(ang), full(anb), full(wcat), full(bs_full), full(qg), full(kg),
            pl.BlockSpec((TOK_TILE, LANES), lambda i: (prev(i) % nt, 0)),
            pl.BlockSpec((TOK_TILE, LANES), lambda i: (prev(i) % nt, 0)),
            full(obd), full(cw), full(cb), full(wg), full(bg), full(lam),
        ],
        out_specs=[
            tok(D_CHUNK),
            pl.BlockSpec((None, None, D_ATTN, TOK_TILE), lambda i: (prev(i) // nt, prev(i) % nt, 0, 0)),
            tok(D_KV),
            pl.BlockSpec((None, N_KV_HEADS, 1, D_KV, TOK_TILE),
                         lambda i: (prev(i) // nt, 0, prev(i) % nt, 0, 0)),
            tok(D_ATTN), tok(D_LRU), coeff_spec, coeff_spec,
        ],
        out_shape=[
            jax.ShapeDtypeStruct((bn, lt, D_CHUNK), BF16),
            jax.ShapeDtypeStruct((bn, nt, D_ATTN, TOK_TILE), BF16),
            jax.ShapeDtypeStruct((bn, lt, D_KV), BF16),
            jax.ShapeDtypeStruct((bn, N_KV_HEADS, nt, D_KV, TOK_TILE), BF16),
            jax.ShapeDtypeStruct((bn, lt, D_ATTN), BF16),
            jax.ShapeDtypeStruct((bn, lt, D_LRU), BF16),
            coeff_shape, coeff_shape,
        ],
        scratch_shapes=[pltpu.VMEM((TOK_TILE, D_IN), F32), pltpu.VMEM((TOK_TILE, D_IN), F32),
                        pltpu.VMEM((SUBLANES, D_LRU), F32)],
        compiler_params=pltpu.CompilerParams(
            dimension_semantics=("arbitrary",), vmem_limit_bytes=VMEM_LIMIT),
        name="in_proj",
    )(*stream, ss, w_in, ang, anb, wcat, bs_full, qg, kg, cos_t, sin_t, obd, cw, cb, wg, bg, lam)


def _sigmoid(t):
    return 0.5 * jnp.tanh(0.5 * t) + 0.5


def _lru_kernel(a_ref, b_ref, gr_ref, o_ref, a_scr, b_scr, p_scr, h_scr):
    lt = gr_ref.shape[0]
    n_lat = lt - CTX_LEN
    n_chunk = lt // LRU_T
    n_ctx_chunk = CTX_LEN // LRU_T
    ctx_rows = LRU_SEG * CTX_PITCH
    lat_rows = LRU_SEG * LAT_PITCH
    assert CTX_LEN <= ctx_rows and n_lat <= lat_rows and (ctx_rows - CTX_LEN) % SUBLANES == 0
    regions = ((0, CTX_PITCH, CTX_LEN, 0), (ctx_rows, LAT_PITCH, n_lat, CTX_LEN))
    slabs = D_LRU // LANES

    for off, pitch, rows, tok0 in regions:
        n_pad = LRU_SEG * pitch - rows
        for d in range(N_DIR):
            for sl in range(slabs):
                a_scr[d, sl, off:off + rows, :] = a_ref[d, sl, tok0:tok0 + rows, :]
                b_scr[d, sl, off:off + rows, :] = b_ref[d, sl, tok0:tok0 + rows, :]
                a_scr[d, sl, off + rows:off + rows + n_pad, :] = jnp.ones((n_pad, LANES), F32)
                b_scr[d, sl, off + rows:off + rows + n_pad, :] = jnp.zeros((n_pad, LANES), F32)

    def sweep(off, pitch):
        def body(i, state):
            new = []
            for d in range(N_DIR):
                t = i if d == 0 else pitch - 1 - i
                idx = pl.ds(off + t, LRU_SEG, stride=pitch)
                for sl in range(slabs):
                    h, p = state[2 * (d * slabs + sl)], state[2 * (d * slabs + sl) + 1]
                    a = a_scr[d, sl, idx, :]
                    h = a * h + b_scr[d, sl, idx, :]
                    p = a * p
                    h_scr[d, sl, idx, :] = h
                    p_scr[d, sl, idx, :] = p
                    new += [h, p]
            return tuple(new)

        ident = (jnp.zeros((LRU_SEG, LANES), F32), jnp.ones((LRU_SEG, LANES), F32)) * (N_DIR * slabs)
        lax.fori_loop(0, pitch, body, ident, unroll=4)

    for off, pitch, _, _ in regions:
        sweep(off, pitch)

    def chain(d, off, pitch, init):
        carries = [None] * LRU_SEG
        c = init
        for j in (range(LRU_SEG) if d == 0 else reversed(range(LRU_SEG))):
            carries[j] = c
            last = off + j * pitch + (pitch - 1 if d == 0 else 0)
            c = [h_scr[d, sl, last:last + 1, :] + p_scr[d, sl, last:last + 1, :] * c[sl]
                 for sl in range(slabs)]
        return carries, c

    zero = [jnp.zeros((1, LANES), F32)] * slabs
    carries = []
    for d in range(N_DIR):
        c_ctx, fin = chain(d, regions[0][0], regions[0][1], zero)
        c_lat, _ = chain(d, regions[1][0], regions[1][1], fin)
        carries.append((c_ctx, c_lat))

    row = lax.broadcasted_iota(jnp.int32, (LRU_T, 1), 0)
    for ci in range(n_chunk):
        reg = 0 if ci < n_ctx_chunk else 1
        off, pitch, _, tok0 = regions[reg]
        r0 = ci * LRU_T - tok0
        j_lo, j_hi = r0 // pitch, (r0 + LRU_T - 1) // pitch
        for sl in range(slabs):
            rows = slice(off + r0, off + r0 + LRU_T)
            h = None
            for d in range(N_DIR):
                seg_c = carries[d][reg]
                c = seg_c[j_hi][sl]
                for j in range(j_hi - 1, j_lo - 1, -1):
                    c = jnp.where(row < (j + 1) * pitch - r0, seg_c[j][sl], c)
                hd = h_scr[d, sl, rows, :] + p_scr[d, sl, rows, :] * c
                h = hd if h is None else h + hd
            cols = slice(sl * LANES, (sl + 1) * LANES)
            gate = gr_ref[ci * LRU_T:(ci + 1) * LRU_T, cols].astype(F32)
            o_ref[ci * LRU_T:(ci + 1) * LRU_T, cols] = (h * gate).astype(BF16)


def _lru_call(a, b, gr):
    bn, lt, _ = gr.shape
    seq = pl.BlockSpec((None, lt, D_LRU), lambda i: (i, 0, 0))
    coeff = pl.BlockSpec((None,) + a.shape[1:], lambda i: (i, 0, 0, 0, 0))
    scan_rows = LRU_SEG * (CTX_PITCH + LAT_PITCH)
    return pl.pallas_call(
        _lru_kernel,
        grid=(bn,),
        in_specs=[coeff, coeff, seq],
        out_specs=seq,
        out_shape=jax.ShapeDtypeStruct((bn, lt, D_LRU), BF16),
        scratch_shapes=[pltpu.VMEM((N_DIR, D_LRU // LANES, scan_rows, LANES), F32)] * 4,
        compiler_params=pltpu.CompilerParams(
            dimension_semantics=("parallel",), vmem_limit_bytes=LRU_VMEM_LIMIT),
        name="rglru",
    )(a, b, gr)


def _attend(qt_ref, k_ref, va_ref, att_scr, rhs_scr, s_scr, p_scr, ot_scr, nk, bound):
    pair = LANES // HEAD_DIM
    width = GQA_GROUP * Q_TILE
    n_chunk = nk // KEY_CHUNK
    units = [(qb, g) for qb in range(qt_ref.shape[1] // Q_TILE) for g in range(N_KV_HEADS)]
    n_unit = len(units)
    zeros = jnp.zeros((HEAD_DIM, width), BF16)

    def stage_rhs(u):
        qb, g = units[u]
        h0 = g * GQA_GROUP
        qt = jnp.concatenate(
            [qt_ref[(h0 + j) * HEAD_DIM:(h0 + j + 1) * HEAD_DIM, qb * Q_TILE:(qb + 1) * Q_TILE]
             for j in range(GQA_GROUP)], axis=1)
        rhs_scr[u % 2] = jnp.concatenate([qt, zeros] if g == 0 else [zeros, qt], axis=0)

    def score_chunk(u, c, mx):
        ks = pl.ds(pl.multiple_of(c * KEY_CHUNK, KEY_CHUNK), KEY_CHUNK)
        s = jnp.dot(k_ref[ks, :], rhs_scr[u % 2], preferred_element_type=F32)
        s_scr[u % 2, c] = s
        for i in range(KEY_CHUNK // SUBLANES):
            mx = jnp.maximum(mx, s[i * SUBLANES:(i + 1) * SUBLANES])
        return mx

    def exp_chunk(u, c, m):
        p_scr[u % 2, c] = jnp.exp2(s_scr[u % 2, c] - m).astype(BF16)

    def pv_chunk(u, c):
        g = units[u][1]
        ot_scr[...] += jnp.dot(va_ref[g, c], p_scr[u % 2, c], preferred_element_type=F32)

    def finish(u):
        qb, g = units[u]
        ot = ot_scr[...]
        if g == 0:
            o, l = ot[0:HEAD_DIM], ot[HEAD_DIM:HEAD_DIM + 1]
        else:
            o, l = ot[HEAD_DIM:2 * HEAD_DIM], ot[0:1]
        o = o * (1.0 / l)
        for rp in range(GQA_GROUP // pair):
            stacked = jnp.concatenate(
                [o[:, (rp * pair + j) * Q_TILE:(rp * pair + j + 1) * Q_TILE] for j in range(pair)], axis=0)
            col = (g * GQA_GROUP + rp * pair) // pair
            att_scr[qb * Q_TILE:(qb + 1) * Q_TILE, col * LANES:(col + 1) * LANES] = stacked.T

    def score_exp_chunk(u, c):
        ks = pl.ds(pl.multiple_of(c * KEY_CHUNK, KEY_CHUNK), KEY_CHUNK)
        s = jnp.dot(k_ref[ks, :], rhs_scr[u % 2], preferred_element_type=F32)
        p_scr[u % 2, c] = jnp.exp2(s - bound).astype(BF16)

    if bound is not None:
        for t in range(n_unit + 1):
            ua, uc = t, t - 1
            do_a, do_c = ua < n_unit, uc >= 0
            if do_a:
                stage_rhs(ua)
            if do_c:
                ot_scr[...] = jnp.zeros(ot_scr.shape, F32)

            def fast_body(c, carry, ua=ua, uc=uc, do_a=do_a, do_c=do_c):
                if do_a:
                    score_exp_chunk(ua, c)
                if do_c:
                    pv_chunk(uc, c)
                return carry

            lax.fori_loop(0, n_chunk, fast_body, 0, unroll=min(PIPE_UNROLL, n_chunk))
            if do_c:
                finish(uc)
        return

    col_max = {}
    for t in range(n_unit + 2):
        ua, ub, uc = t, t - 1, t - 2
        do_a, do_b, do_c = ua < n_unit, 0 <= ub < n_unit, 0 <= uc < n_unit
        if do_a:
            stage_rhs(ua)
        if do_c:
            ot_scr[...] = jnp.zeros(ot_scr.shape, F32)

        def body(c, mx, ua=ua, ub=ub, uc=uc, do_a=do_a, do_b=do_b, do_c=do_c):
            if do_a:
                mx = score_chunk(ua, c, mx)
            if do_b:
                exp_chunk(ub, c, col_max[ub])
            if do_c:
                pv_chunk(uc, c)
            return mx

        mx = lax.fori_loop(0, n_chunk, body, jnp.full((SUBLANES, width), -jnp.inf, F32),
                           unroll=min(PIPE_UNROLL, n_chunk))
        if do_a:
            col_max[ua] = jnp.max(mx, axis=0, keepdims=True)
        if do_c:
            finish(uc)


def _attn_kernel(flag_ref, bnd_ref, qt_ref, k_ref, va_ref, oa_ref, gb_ref, ol_ref, *refs,
                 q_off, with_ctx, split):
    if split:
        x_ref, ctx_ref, *refs = refs
    else:
        x_ref, *refs = refs
        ctx_ref = None
    mod_ref, wo_ref, lng_ref, lnb_ref, o_ref, att_scr, rhs_scr, s_scr, p_scr, ot_scr = refs
    nk_all = k_ref.shape[0]
    scratch = (att_scr, rhs_scr, s_scr, p_scr, ot_scr)
    bounded = flag_ref[0] == 1
    is_ctx = (pl.program_id(1) + q_off) < (CTX_LEN // Q_STEP)

    for ctx_step, nk in ((True, CTX_LEN), (False, nk_all)):
        if ctx_step and not with_ctx:
            continue
        here = (is_ctx if ctx_step else jnp.logical_not(is_ctx)) if with_ctx else True

        @pl.when(jnp.logical_and(here, bounded))
        def _():
            _attend(qt_ref, k_ref, va_ref, *scratch, nk, bnd_ref[...])

        @pl.when(jnp.logical_and(here, jnp.logical_not(bounded)))
        def _():
            _attend(qt_ref, k_ref, va_ref, *scratch, nk, None)

    att = (att_scr[...] * gb_ref[...].astype(F32)).astype(BF16)
    cat = jnp.concatenate([oa_ref[...], att, ol_ref[...]], axis=1)
    y = jnp.dot(cat, wo_ref[...], preferred_element_type=F32)
    gate = mod_ref[:, 2 * D_MODEL:3 * D_MODEL]
    xt = _stream_tile(x_ref, ctx_ref, is_ctx)
    o_ref[...] = _layer_norm(ALPHA * xt + gate * y, lng_ref[...], lnb_ref[...])


def _attn_call(flag, bnd, qt, k, va, oa, gb, ol, stream, ss, w_o, layer, lng, lnb, last):
    split = isinstance(stream, tuple)
    assert not (split and last) and Q_STEP == TOK_TILE
    bn, lt, _ = oa.shape
    n_ctx_step = CTX_LEN // Q_STEP
    q_off = n_ctx_step if last else 0
    nq = lt // Q_STEP - q_off
    tok = lambda width: pl.BlockSpec((None, Q_STEP, width), lambda b, i: (b, i + q_off, 0))
    full = lambda a: pl.BlockSpec(a.shape, lambda b, i: (0,) * a.ndim)
    if split:
        stream_specs = [
            pl.BlockSpec((None, Q_STEP, D_MODEL), lambda b, i: (b, jnp.maximum(i - 1, 0), 0)),
            pl.BlockSpec((None, CTX_LEN, D_MODEL), lambda b, i: (b, 0, 0)),
        ]
    else:
        stream = (stream,)
        stream_specs = [tok(D_MODEL)]
    return pl.pallas_call(
        functools.partial(_attn_kernel, q_off=q_off, with_ctx=not last, split=split),
        grid=(bn, nq),
        in_specs=[
            pl.BlockSpec(memory_space=pltpu.SMEM),
            full(bnd),
            pl.BlockSpec((None, None, D_ATTN, Q_STEP), lambda b, i: (b, i + q_off, 0, 0)),
            pl.BlockSpec((None, lt, D_KV), lambda b, i: (b, 0, 0)),
            pl.BlockSpec((None, N_KV_HEADS, lt // KEY_CHUNK, D_KV, KEY_CHUNK), lambda b, i: (b, 0, 0, 0, 0)),
            tok(D_CHUNK), tok(D_ATTN), tok(D_LRU),
        ] + stream_specs + [
            pl.BlockSpec((None, None, 1, 3 * D_MODEL),
                         lambda b, i: (b, jnp.minimum((i + q_off) // n_ctx_step, 1), 0, 0)),
            pl.BlockSpec((None,) + w_o.shape[1:], lambda b, i: (layer, 0, 0)),
            full(lng), full(lnb),
        ],
        out_specs=pl.BlockSpec((None, Q_STEP, D_MODEL), lambda b, i: (b, i, 0)),
        out_shape=jax.ShapeDtypeStruct((bn, nq * Q_STEP, D_MODEL), F32),
        scratch_shapes=[
            pltpu.VMEM((Q_STEP, D_ATTN), F32),
            pltpu.VMEM((2, D_KV, GQA_GROUP * Q_TILE), BF16),
            pltpu.VMEM((2, lt // KEY_CHUNK, KEY_CHUNK, GQA_GROUP * Q_TILE), F32),
            pltpu.VMEM((2, lt // KEY_CHUNK, KEY_CHUNK, GQA_GROUP * Q_TILE), BF16),
            pltpu.VMEM((D_KV, GQA_GROUP * Q_TILE), F32),
        ],
        compiler_params=pltpu.CompilerParams(
            dimension_semantics=("parallel", "arbitrary"), vmem_limit_bytes=VMEM_LIMIT),
        name="attn_merge",
    )(flag, bnd, qt, k, va, oa, gb, ol, *stream, ss, w_o, lng, lnb)


def _rope_tables(n_lat):
    nf = HEAD_DIM // 4
    t = jnp.arange(n_lat, dtype=jnp.int32)
    pos = jnp.stack([t // GRID_W, t % GRID_W], axis=1).astype(F32)
    inv = ROPE_THETA ** (-jnp.arange(nf, dtype=F32) / nf)
    d = np.arange(HEAD_DIM)
    ang = pos[:, d // (HEAD_DIM // 2)] * inv[d % nf]
    sign = jnp.asarray(np.where((d % (HEAD_DIM // 2)) < nf, -1.0, 1.0), F32)
    cos = jnp.concatenate([jnp.ones((CTX_LEN, HEAD_DIM), F32), jnp.cos(ang)], axis=0)
    sin = jnp.concatenate([jnp.zeros((CTX_LEN, HEAD_DIM), F32), jnp.sin(ang) * sign], axis=0)
    reps = LANES // HEAD_DIM
    return jnp.tile(cos, (1, reps)), jnp.tile(sin, (1, reps))


def _block_diag(w):
    nb, n, _ = w.shape
    eye = jnp.eye(nb, dtype=w.dtype)
    return (eye[:, None, :, None] * w[:, :, None, :]).reshape(nb * n, nb * n)


def kernel(x, c, ctx, c_ctx, w_ada, b_ada, w_in, a_norm_g, a_norm_b, a_ws, a_bs, q_norm_g, k_norm_g,
           conv_w, conv_b, lru_wr, lru_br, lru_wi, lru_bi, lru_lam, w_o, ln_g, ln_b):
    bn, n_lat, _ = x.shape
    assert ctx.shape[1] == CTX_LEN and n_lat % TOK_TILE == 0 and bn + 1 <= MOD_ROWS
    xa = (x, ctx)

    cc = jnp.zeros((MOD_ROWS, D_MODEL), F32).at[:bn].set(c).at[bn].set(c_ctx)
    mod = _ada_call(cc, w_ada, b_ada)
    ctx_mod = jnp.broadcast_to(mod[:, bn:bn + 1], (DEPTH, bn, 3 * D_MODEL))
    ss = jnp.stack([ctx_mod, mod[:, :bn]], axis=2).reshape(DEPTH, bn, 2, 1, 3 * D_MODEL)

    cos_t, sin_t = _rope_tables(n_lat)
    head = np.arange(LANES) // HEAD_DIM
    obd = jnp.asarray(np.tile(head[:, None] == head[None, :], (2, 1)), BF16)

    w_in_b = w_in.astype(BF16)
    w_o_b = w_o.astype(BF16)
    for l in range(DEPTH):
        wcat = jnp.concatenate([a_ws[l, g] for g in range(A_GROUPS)], axis=1).astype(BF16)
        bs_full = jnp.repeat(a_bs[l].T, A_GDIM, axis=1)
        qg = (jnp.tile(q_norm_g[l], N_HEADS) * (HEAD_DIM ** -0.5 * LOG2E)).reshape(1, D_ATTN)
        kg = jnp.tile(k_norm_g[l], N_KV_HEADS).reshape(1, D_KV)
        wg = jnp.concatenate(
            [m for d in range(N_DIR) for m in (_block_diag(lru_wr[l, d]), _block_diag(lru_wi[l, d]))],
            axis=1).astype(BF16)
        bg = jnp.concatenate(
            [m for d in range(N_DIR) for m in (lru_br[l, d], lru_bi[l, d])]).reshape(1, 4 * D_LRU)
        oa, qt, k, va, gb, gr, lru_a, lru_b = _in_call(
            xa, ss[l], w_in_b, l, a_norm_g[l].reshape(1, D_CHUNK), a_norm_b[l].reshape(1, D_CHUNK),
            wcat, bs_full, qg, kg, cos_t, sin_t, obd,
            conv_w[l], conv_b[l].reshape(1, D_LRU), wg, bg, lru_lam[l])
        ol = _lru_call(lru_a, lru_b, gr)

        s_bound = (SCORE_BOUND_SLACK * HEAD_DIM ** 0.5 * LOG2E
                   * jnp.max(jnp.abs(q_norm_g[l])) * jnp.max(jnp.abs(k_norm_g[l])))
        flag = (s_bound <= MAX_SAFE_SCORE_BOUND).astype(jnp.int32).reshape(1)
        bnd = jnp.full((1, GQA_GROUP * Q_TILE), s_bound, F32)
        xa = _attn_call(flag, bnd, qt, k, va, oa, gb, ol, xa, ss[l], w_o_b, l, ln_g[l].reshape(1, D_MODEL),
                        ln_b[l].reshape(1, D_MODEL), last=(l == DEPTH - 1))
    return xa
```
